```python
import jax, jax.numpy as jnp
from jax import lax
import numpy as np

D_MODEL = 1024
BATCH = 16
SEQ = 2048
DEPTH = 1

HEAD_DIM = 64
ROPE_THETA = 10000.0
NORM_EPS = 1e-6
NEG_INF = -1e30
BIG = 1e4
NSA_HEADS = 8
NSA_KV_HEADS = 2
NSA_GROUP = NSA_HEADS // NSA_KV_HEADS
CMP_BLOCK = 32
CMP_STRIDE = 16
CMP_HIDDEN = 256
SLC_BLOCK = 64
SLC_TOPN = 8
SLC_LOCAL = 2
WINDOW = 512
NSA_QCHUNK = 64
DSA_HEADS = 8
IDX_HEADS = 4
IDX_DIM = 64
DSA_TOPK_MAX = 256
DSA_QCHUNK = 128
D_FF = -(-8 * D_MODEL // (3 * 256)) * 256
IN_SPLIT = (
    NSA_HEADS * HEAD_DIM,
    NSA_KV_HEADS * HEAD_DIM,
    NSA_KV_HEADS * HEAD_DIM,
    NSA_KV_HEADS * HEAD_DIM,
    NSA_KV_HEADS * HEAD_DIM,
    NSA_KV_HEADS * HEAD_DIM,
    NSA_KV_HEADS * HEAD_DIM,
    3 * NSA_HEADS,
    DSA_HEADS * HEAD_DIM,
    HEAD_DIM,
    HEAD_DIM,
    IDX_HEADS * IDX_DIM,
    IDX_DIM,
    IDX_HEADS,
    2 * D_MODEL,
)
D_IN = sum(IN_SPLIT)

kernel_name = 'hybrid_nsa_dsa_gated_block'


def rmsnorm(x, g):
    xf = x.astype(jnp.float32)
    y = xf * lax.rsqrt(jnp.mean(xf * xf, axis=-1, keepdims=True) + NORM_EPS)
    return (y * g.astype(jnp.float32)).astype(x.dtype)


def rope(x, positions):
    half = x.shape[-1] // 2
    inv_freq = ROPE_THETA ** (-jnp.arange(half, dtype=jnp.float32) / half)
    ang = positions.astype(jnp.float32)[:, None] * inv_freq[None, :]
    cos = jnp.cos(ang)[None, :, None, :]
    sin = jnp.sin(ang)[None, :, None, :]
    xf = x.astype(jnp.float32)
    x1, x2 = xf[..., :half], xf[..., half:]
    return jnp.concatenate([x1 * cos - x2 * sin, x1 * sin + x2 * cos], axis=-1).astype(x.dtype)


def masked_softmax(s, mask):
    s = jnp.where(mask, s.astype(jnp.float32), NEG_INF)
    p = jax.nn.softmax(s, axis=-1)
    return p * jnp.any(mask, axis=-1, keepdims=True)


def to_chunks(a, size):
    b, s = a.shape[0], a.shape[1]
    return a.reshape(b, s // size, size, *a.shape[2:]).swapaxes(0, 1)


def from_chunks(a):
    a = a.swapaxes(0, 1)
    return a.reshape(a.shape[0], a.shape[1] * a.shape[2], *a.shape[3:])


def compress_tokens(kv, pos_emb, w1, w2):
    b, s, hk, d = kv.shape
    n_rep = CMP_BLOCK // CMP_STRIDE
    n_chunks = s // CMP_STRIDE
    ch = kv.reshape(b, n_chunks, CMP_STRIDE, hk, d)
    blocks = jnp.concatenate([ch[:, r:n_chunks - n_rep + 1 + r] for r in range(n_rep)], axis=2)
    blocks = blocks + pos_emb[None, None, :, None, :]
    flat = blocks.transpose(0, 1, 3, 2, 4).reshape(b, blocks.shape[1], hk, CMP_BLOCK * d)
    return jax.nn.gelu(flat @ w1) @ w2


def nsa_mixer(q, k_cmp, v_cmp, k_slc, v_slc, k_win, v_win, gates, pos_k, w1k, w2k, pos_v, w1v, w2v):
    b, s, h, d = q.shape
    scale = d ** -0.5
    qg = q.reshape(b, s, NSA_KV_HEADS, NSA_GROUP, d)
    t = jnp.arange(s)
    kc = compress_tokens(k_cmp, pos_k, w1k, w2k)
    vc = compress_tokens(v_cmp, pos_v, w1v, w2v)
    n_cmp = kc.shape[1]
    cmp_end = jnp.arange(n_cmp) * CMP_STRIDE + CMP_BLOCK - 1
    cmp_mask = (cmp_end[None, :] <= t[:, None])[None, :, None, None, :]
    p_cmp = masked_softmax(jnp.einsum('bshgd,bnhd->bshgn', qg, kc) * scale, cmp_mask)
    o_cmp = jnp.einsum('bshgn,bnhd->bshgd', p_cmp.astype(vc.dtype), vc)
    n_slc = s // SLC_BLOCK
    ci = jnp.arange(n_cmp)[:, None] * CMP_STRIDE
    sj = jnp.arange(n_slc)[None, :] * SLC_BLOCK
    overlap = ((ci < sj + SLC_BLOCK) & (ci + CMP_BLOCK > sj)).astype(jnp.float32)
    p_slc = jnp.einsum('bshn,nj->bshj', p_cmp.sum(axis=3), overlap)
    cur = (t // SLC_BLOCK)[:, None]
    j = jnp.arange(n_slc)[None, :]
    forced = (j == 0) | ((cur - j >= 0) & (cur - j < SLC_LOCAL))
    admissible = j <= cur
    blk_score = jnp.where(forced[None, :, None, :], BIG, p_slc)
    blk_score = jnp.where(admissible[None, :, None, :], blk_score, NEG_INF)
    n_sel = min(SLC_TOPN, n_slc)
    _, blk_idx = lax.top_k(blk_score, n_sel)
    kb = k_slc.reshape(b, n_slc, SLC_BLOCK, NSA_KV_HEADS, d).transpose(0, 3, 1, 2, 4)
    vb = v_slc.reshape(b, n_slc, SLC_BLOCK, NSA_KV_HEADS, d).transpose(0, 3, 1, 2, 4)
    k_win_pad = jnp.pad(k_win, ((0, 0), (WINDOW, 0), (0, 0), (0, 0)))
    v_win_pad = jnp.pad(v_win, ((0, 0), (WINDOW, 0), (0, 0), (0, 0)))
    bi = jnp.arange(b)[:, None, None, None]
    hi = jnp.arange(NSA_KV_HEADS)[None, None, :, None]
    n_tok = n_sel * SLC_BLOCK

    def chunk(args):
        q_c, idx_c, c = args
        t_c = c * NSA_QCHUNK + jnp.arange(NSA_QCHUNK)
        k_sel = kb[bi, hi, idx_c].reshape(b, NSA_QCHUNK, NSA_KV_HEADS, n_tok, d)
        v_sel = vb[bi, hi, idx_c].reshape(b, NSA_QCHUNK, NSA_KV_HEADS, n_tok, d)
        tok = (idx_c[..., None] * SLC_BLOCK + jnp.arange(SLC_BLOCK)).reshape(b, NSA_QCHUNK, NSA_KV_HEADS, n_tok)
        sel_mask = (tok <= t_c[None, :, None, None])[:, :, :, None, :]
        p = masked_softmax(jnp.einsum('bqhgd,bqhkd->bqhgk', q_c, k_sel) * scale, sel_mask)
        o_slc = jnp.einsum('bqhgk,bqhkd->bqhgd', p.astype(v_sel.dtype), v_sel)
        start = c * NSA_QCHUNK
        kwin = lax.dynamic_slice_in_dim(k_win_pad, start, WINDOW + NSA_QCHUNK, axis=1)
        vwin = lax.dynamic_slice_in_dim(v_win_pad, start, WINDOW + NSA_QCHUNK, axis=1)
        kpos = start - WINDOW + jnp.arange(WINDOW + NSA_QCHUNK)
        dist = t_c[:, None] - kpos[None, :]
        win_mask = ((kpos[None, :] >= 0) & (dist >= 0) & (dist < WINDOW))[None, :, None, None, :]
        p = masked_softmax(jnp.einsum('bqhgd,bkhd->bqhgk', q_c, kwin) * scale, win_mask)
        o_win = jnp.einsum('bqhgk,bkhd->bqhgd', p.astype(vwin.dtype), vwin)
        return o_slc, o_win

    n_chunk = s // NSA_QCHUNK
    o_slc, o_win = lax.map(chunk, (to_chunks(qg, NSA_QCHUNK), to_chunks(blk_idx, NSA_QCHUNK), jnp.arange(n_chunk)))
    g = jax.nn.sigmoid(gates.astype(jnp.float32)).astype(q.dtype).reshape(b, s, 3, NSA_KV_HEADS, NSA_GROUP)[..., None]
    o = g[:, :, 0] * o_cmp + g[:, :, 1] * from_chunks(o_slc) + g[:, :, 2] * from_chunks(o_win)
    return o.reshape(b, s, h * d)


def dsa_mixer(q, k, v, iq, ik, iw):
    b, s, h, d = q.shape
    scale = d ** -0.5
    topk = min(DSA_TOPK_MAX, s // 4)
    key_pos = jnp.arange(s)
    bi = jnp.arange(b)[:, None, None]

    def chunk(args):
        q_c, iq_c, iw_c, c = args
        t_c = c * DSA_QCHUNK + jnp.arange(DSA_QCHUNK)
        idx_logits = jnp.einsum('bqhd,bsd->bqhs', iq_c, ik).astype(jnp.float32) * IDX_DIM ** -0.5
        score = jnp.einsum('bqh,bqhs->bqs', iw_c.astype(jnp.float32), jax.nn.relu(idx_logits))
        score = jnp.where(key_pos[None, None, :] <= t_c[None, :, None], score, NEG_INF)
        _, sel = lax.top_k(score, topk)
        k_sel = k[bi, sel]
        v_sel = v[bi, sel]
        mask = (sel <= t_c[None, :, None])[:, :, None, :]
        p = masked_softmax(jnp.einsum('bqhd,bqkd->bqhk', q_c, k_sel) * scale, mask)
        return jnp.einsum('bqhk,bqkd->bqhd', p.astype(v_sel.dtype), v_sel)

    n_chunk = s // DSA_QCHUNK
    o = lax.map(chunk, (to_chunks(q, DSA_QCHUNK), to_chunks(iq, DSA_QCHUNK), to_chunks(iw, DSA_QCHUNK), jnp.arange(n_chunk)))
    return from_chunks(o).reshape(b, s, h * d)


def setup_inputs(seed: int = 0) -> dict:
    key = jax.random.key(seed)
    ks = jax.random.split(key, 17)
    f32 = jnp.float32

    def w(k, shape, fan_in):
        return jax.random.normal(k, shape, f32) * fan_in ** -0.5

    def gain(k, shape):
        return 1.0 + 0.05 * jax.random.normal(k, shape, f32)

    return {
        'x': jax.random.normal(ks[0], (BATCH, SEQ, D_MODEL), f32),
        'norm_mix': gain(ks[1], (DEPTH, D_MODEL)),
        'w_in': w(ks[2], (DEPTH, D_MODEL, D_IN), D_MODEL),
        'cmp_pos_k': 0.1 * jax.random.normal(ks[3], (DEPTH, CMP_BLOCK, HEAD_DIM), f32),
        'cmp_w1_k': w(ks[4], (DEPTH, CMP_BLOCK * HEAD_DIM, CMP_HIDDEN), CMP_BLOCK * HEAD_DIM),
        'cmp_w2_k': w(ks[5], (DEPTH, CMP_HIDDEN, HEAD_DIM), CMP_HIDDEN),
        'cmp_pos_v': 0.1 * jax.random.normal(ks[6], (DEPTH, CMP_BLOCK, HEAD_DIM), f32),
        'cmp_w1_v': w(ks[7], (DEPTH, CMP_BLOCK * HEAD_DIM, CMP_HIDDEN), CMP_BLOCK * HEAD_DIM),
        'cmp_w2_v': w(ks[8], (DEPTH, CMP_HIDDEN, HEAD_DIM), CMP_HIDDEN),
        'w_branch_nsa': w(ks[9], (DEPTH, NSA_HEADS * HEAD_DIM, D_MODEL), NSA_HEADS * HEAD_DIM),
        'w_branch_dsa': w(ks[10], (DEPTH, DSA_HEADS * HEAD_DIM, D_MODEL), DSA_HEADS * HEAD_DIM),
        'w_out': w(ks[11], (DEPTH, D_MODEL, D_MODEL), D_MODEL),
        'norm_ffn': gain(ks[12], (DEPTH, D_MODEL)),
        'w_gate': w(ks[13], (DEPTH, D_MODEL, D_FF), D_MODEL),
        'w_up': w(ks[14], (DEPTH, D_MODEL, D_FF), D_MODEL),
        'w_down': w(ks[15], (DEPTH, D_FF, D_MODEL), D_FF),
        'norm_final': gain(ks[16], (D_MODEL,)),
    }


def reference(x, norm_mix, w_in, cmp_pos_k, cmp_w1_k, cmp_w2_k, cmp_pos_v, cmp_w1_v, cmp_w2_v, w_branch_nsa, w_branch_dsa, w_out, norm_ffn, w_gate, w_up, w_down, norm_final):
    b, s, _ = x.shape
    positions = jnp.arange(s)
    offsets = np.cumsum(IN_SPLIT)[:-1].tolist()
    h = x
    for layer in range(DEPTH):
        xn = rmsnorm(h, norm_mix[layer])
        z = xn @ w_in[layer]
        (q_a, kc, vc, ksl, vsl, kwn, vwn, g_a, q_b, k_b, v_b, iq, ik, iw, g_merge) = jnp.split(z, offsets, axis=-1)
        kvh = lambda a: a.reshape(b, s, NSA_KV_HEADS, HEAD_DIM)
        q_a = rope(q_a.reshape(b, s, NSA_HEADS, HEAD_DIM), positions)
        kc = rope(kvh(kc), positions)
        ksl = rope(kvh(ksl), positions)
        kwn = rope(kvh(kwn), positions)
        y_a = nsa_mixer(q_a, kc, kvh(vc), ksl, kvh(vsl), kwn, kvh(vwn), g_a,
                        cmp_pos_k[layer], cmp_w1_k[layer], cmp_w2_k[layer],
                        cmp_pos_v[layer], cmp_w1_v[layer], cmp_w2_v[layer])
        q_b = rope(q_b.reshape(b, s, DSA_HEADS, HEAD_DIM), positions)
        k_b = rope(k_b.reshape(b, s, 1, HEAD_DIM), positions)[:, :, 0]
        iq = rope(iq.reshape(b, s, IDX_HEADS, IDX_DIM), positions)
        ik = rope(ik.reshape(b, s, 1, IDX_DIM), positions)[:, :, 0]
        iw = iw * IDX_HEADS ** -0.5
        y_b = dsa_mixer(q_b, k_b, v_b, iq, ik, iw)
        y_a = y_a @ w_branch_nsa[layer]
        y_b = y_b @ w_branch_dsa[layer]
        gate = jax.nn.sigmoid(g_merge.astype(jnp.float32)).astype(h.dtype)
        merged = gate[..., :D_MODEL] * y_a + gate[..., D_MODEL:] * y_b
        h = h + merged @ w_out[layer]
        hn = rmsnorm(h, norm_ffn[layer])
        h = h + (jax.nn.silu(hn @ w_gate[layer]) * (hn @ w_up[layer])) @ w_down[layer]
    return rmsnorm(h, norm_final)
```

```python
import functools

import numpy as np
import jax
import jax.numpy as jnp
from jax import lax
from jax.experimental import pallas as pl
from jax.experimental.pallas import tpu as pltpu

F32 = jnp.float32
BF16 = jnp.bfloat16

D_MODEL = 1024
HEAD_DIM = 64
ROPE_THETA = 10000.0
NORM_EPS = 1e-6
NEG_INF = -1e30
BIG = 1e4
_FMAX = 3.0e38
NSA_HEADS = 8
NSA_KV_HEADS = 2
NSA_GROUP = NSA_HEADS // NSA_KV_HEADS
CMP_BLOCK = 32
CMP_STRIDE = 16
CMP_HIDDEN = 256
SLC_BLOCK = 64
SLC_TOPN = 8
SLC_LOCAL = 2
WINDOW = 512
DSA_HEADS = 8
IDX_HEADS = 4
IDX_DIM = 64
DSA_TOPK_MAX = 256
D_FF = -(-8 * D_MODEL // (3 * 256)) * 256

LANES = 128
VMEM_LIMIT = 56 * 1024 * 1024

_O_QA, _O_KC, _O_VC, _O_KSL, _O_VSL, _O_KWN, _O_VWN = 0, 512, 640, 768, 896, 1024, 1152
_O_GA, _O_QB, _O_KB, _O_VB, _O_IQ, _O_IK, _O_IW, _O_GM = 1280, 1304, 1816, 1880, 1944, 2200, 2264, 2268
_D_IN = 4316
_N_QA, _N_KVC, _N_KVS, _N_KVW, _N_QB, _N_KVB, _N_IQ, _N_IK, _N_MISC, _N_GM = (
    0, 512, 768, 1024, 1280, 1792, 1920, 2176, 2304, 2432)
_D_IN_PAD = 4480
_MISC_GA, _MISC_IW = 0, 24


def _dot(a, b):
    return jnp.dot(a, b, preferred_element_type=F32)


def _dot_nt(a, b):
    return lax.dot_general(a, b, (((1,), (1,)), ((), ())), preferred_element_type=F32)


def _sigmoid(x):
    return 1.0 / (1.0 + jnp.exp(-x))


def _stack_heads(x, n_heads):
    tq = x.shape[0]
    low = lax.broadcasted_iota(jnp.int32, (tq, LANES), 1) < HEAD_DIM
    parts = []
    for p in range(n_heads // 2):
        slab = x[:, LANES * p:LANES * (p + 1)]
        parts.append(jnp.where(low, slab, 0.0))
        parts.append(jnp.where(low, pltpu.roll(slab, HEAD_DIM, 1), 0.0))
    return jnp.concatenate(parts, axis=0).astype(BF16)


def _unstack_heads(acc, n_heads):
    tq = acc.shape[0] // n_heads
    low = lax.broadcasted_iota(jnp.int32, (tq, LANES), 1) < HEAD_DIM
    outs = []
    for p in range(n_heads // 2):
        a0 = acc[(2 * p) * tq:(2 * p + 1) * tq]
        a1 = acc[(2 * p + 1) * tq:(2 * p + 2) * tq]
        outs.append(jnp.where(low, pltpu.roll(a0, HEAD_DIM, 1), a1))
    return jnp.concatenate(outs, axis=1)


def _softmax_step(s, mask, kv, m, l, acc):
    h, tq, tk = s.shape
    s = jnp.where(mask, s, NEG_INF)
    m_new = jnp.maximum(m, jnp.max(s, axis=2, keepdims=True))
    alpha = jnp.exp(m - m_new)
    p = jnp.where(mask, jnp.exp(s - m_new), 0.0)
    l_new = alpha * l + jnp.sum(p, axis=2, keepdims=True)
    pv = _dot(p.reshape(h * tq, tk).astype(BF16), kv)
    acc_new = (alpha * acc.reshape(h, tq, LANES)).reshape(h * tq, LANES) + pv
    return m_new, l_new, acc_new


def _softmax_finish(l, acc):
    h, tq, _ = l.shape
    inv = jnp.where(l > 0.0, 1.0 / l, 0.0)
    return (acc.reshape(h, tq, LANES) * inv).reshape(h * tq, LANES)


def _proj_kernel(x_ref, g_ref, w_ref, tab_ref, qa_ref, kvc_ref, kvs_ref, kvw_ref, qb_ref, kvb_ref,
                 iq_ref, ik_ref, misc_ref, gm_ref):
    x = x_ref[...]
    tm = x.shape[0]
    ms = jnp.mean(x * x, axis=-1, keepdims=True)
    xn = (x * lax.rsqrt(ms + NORM_EPS) * g_ref[...]).astype(BF16)
    low32 = (lax.broadcasted_iota(jnp.int32, (tm, LANES), 1) & (HEAD_DIM // 2)) == 0

    def rope(z, kind):
        cos = tab_ref[2 * kind]
        sin = tab_ref[2 * kind + 1]
        rot = jnp.where(low32, pltpu.roll(z, LANES - HEAD_DIM // 2, 1), pltpu.roll(z, HEAD_DIM // 2, 1))
        return z * cos + rot * sin

    def emit(col, width, out_ref, out_col, kinds):
        z = _dot(xn, w_ref[:, col:col + width])
        for j in range(width // LANES):
            zj = z[:, LANES * j:LANES * (j + 1)]
            if kinds[j] is not None:
                zj = rope(zj, kinds[j])
            out_ref[:, out_col + LANES * j:out_col + LANES * (j + 1)] = zj.astype(out_ref.dtype)

    ROPE_Q, ROPE_K, ROPE_KV = 0, 1, 2
    emit(_N_QA, 512, qa_ref, 0, [ROPE_Q] * 4)
    emit(_N_KVC, 256, kvc_ref, 0, [ROPE_K, None])
    emit(_N_KVS, 256, kvs_ref, 0, [ROPE_KV] * 2)
    emit(_N_KVW, 256, kvw_ref, 0, [ROPE_KV] * 2)
    emit(_N_QB, 512, qb_ref, 0, [ROPE_Q] * 4)
    emit(_N_KVB, 128, kvb_ref, 0, [ROPE_KV])
    emit(_N_IQ, 256, iq_ref, 0, [ROPE_Q] * 2)
    emit(_N_IK, 128, ik_ref, 0, [ROPE_K])
    emit(_N_MISC, 128, misc_ref, 0, [None])
    for c in range(4):
        emit(_N_GM + 512 * c, 512, gm_ref, 512 * c, [None] * 4)


def _relayout_w_in(w):
    z = lambda n: jnp.zeros((w.shape[0], n), w.dtype)
    s = lambda a, n: w[:, a:a + n]
    cols = [
        s(_O_QA, 512), s(_O_KC, 128), s(_O_VC, 128),
        s(_O_KSL, 64), s(_O_VSL, 64), s(_O_KSL + 64, 64), s(_O_VSL + 64, 64),
        s(_O_KWN, 64), s(_O_VWN, 64), s(_O_KWN + 64, 64), s(_O_VWN + 64, 64),
        s(_O_QB, 512), s(_O_KB, 64), s(_O_VB, 64), s(_O_IQ, 256),
        s(_O_IK, 64), z(64),
        s(_O_GA, 24), s(_O_IW, 4), z(100),
        s(_O_GM, 2048),
    ]
    out = jnp.concatenate(cols, axis=1)
    assert out.shape[1] == _D_IN_PAD
    return out.astype(BF16)


def _rope_tables(seq):
    half = HEAD_DIM // 2
    inv_freq = ROPE_THETA ** (-jnp.arange(half, dtype=F32) / half)
    ang = jnp.arange(seq, dtype=F32)[:, None] * inv_freq[None, :]
    cos, sin = jnp.cos(ang), jnp.sin(ang)
    cos64 = jnp.concatenate([cos, cos], axis=1)
    sin64 = jnp.concatenate([-sin, sin], axis=1)
    cos_k = jnp.concatenate([cos64, cos64], axis=1)
    sin_k = jnp.concatenate([sin64, sin64], axis=1)
    scale = HEAD_DIM ** -0.5
    cos_kv = jnp.concatenate([cos64, jnp.ones_like(cos64)], axis=1)
    sin_kv = jnp.concatenate([sin64, jnp.zeros_like(sin64)], axis=1)
    return jnp.stack([cos_k * scale, sin_k * scale, cos_k, sin_k, cos_kv, sin_kv], axis=0)


def _proj(x2, gain, w_pad, tabs, seq, tm):
    n = x2.shape[0]
    nblk_seq = seq // tm
    row = lambda width: pl.BlockSpec((tm, width), lambda i: (i, 0))
    full = lambda shape: pl.BlockSpec(shape, lambda i: (0,) * len(shape))
    widths = [(512, BF16), (256, F32), (256, BF16), (256, BF16), (512, BF16), (128, BF16), (256, BF16),
              (128, BF16), (128, F32), (2048, F32)]
    return pl.pallas_call(
        _proj_kernel,
        grid=(n // tm,),
        in_specs=[row(D_MODEL), full((1, D_MODEL)), full((D_MODEL, _D_IN_PAD)),
                  pl.BlockSpec((6, tm, LANES), lambda i: (0, i % nblk_seq, 0))],
        out_specs=[row(wd) for wd, _ in widths],
        out_shape=[jax.ShapeDtypeStruct((n, wd), dt) for wd, dt in widths],
        compiler_params=pltpu.CompilerParams(dimension_semantics=("arbitrary",), vmem_limit_bytes=VMEM_LIMIT),
    )(x2, gain, w_pad, tabs)


def _compress_kernel(c_ref, pos_ref, w1_ref, w2_ref, out_ref):
    n_chunk = c_ref.shape[3]
    for h in range(NSA_KV_HEADS):
        acc = jnp.zeros((n_chunk, LANES), F32)
        for kv in range(2):
            c = c_ref[0, kv, h]
            a_lo = (c + pos_ref[kv, 0]).astype(BF16)
            a_hi = (c + pos_ref[kv, 1]).astype(BF16)
            half = CMP_STRIDE * HEAD_DIM
            h_lo = _dot(a_lo, w1_ref[kv, :half, :])
            h_hi = _dot(a_hi, w1_ref[kv, half:, :])
            hid = h_lo + pltpu.roll(h_hi, n_chunk - 1, 0)
            act = jax.nn.gelu(hid, approximate=True).astype(BF16)
            acc = acc + _dot(act, w2_ref[kv])
        out_ref[0, h] = acc.astype(out_ref.dtype)


def _compress(c, pos, w1, w2):
    b = c.shape[0]
    n_chunk = c.shape[3]
    return pl.pallas_call(
        _compress_kernel,
        grid=(b,),
        in_specs=[pl.BlockSpec((1,) + c.shape[1:], lambda i: (i, 0, 0, 0, 0)),
                  pl.BlockSpec(pos.shape, lambda i: (0, 0, 0, 0)),
                  pl.BlockSpec(w1.shape, lambda i: (0, 0, 0)),
                  pl.BlockSpec(w2.shape, lambda i: (0, 0, 0))],
        out_specs=pl.BlockSpec((1, NSA_KV_HEADS, n_chunk, LANES), lambda i: (i, 0, 0, 0)),
        out_shape=jax.ShapeDtypeStruct((b, NSA_KV_HEADS, n_chunk, LANES), BF16),
        compiler_params=pltpu.CompilerParams(dimension_semantics=("arbitrary",), vmem_limit_bytes=VMEM_LIMIT),
    )(c, pos, w1, w2)


def _cmp_kernel(qa_ref, kvc_ref, ot_ref, ocmp_ref, selt_ref):
    tq = qa_ref.shape[1]
    n_cmp = kvc_ref.shape[2]
    n_slc = ot_ref.shape[0]
    q0 = pl.program_id(1) * tq
    q = qa_ref[0].astype(F32)
    t3 = q0 + lax.broadcasted_iota(jnp.int32, (NSA_GROUP, tq, n_cmp), 1)
    n3 = lax.broadcasted_iota(jnp.int32, (NSA_GROUP, tq, n_cmp), 2)
    mask = (n3 * CMP_STRIDE + (CMP_BLOCK - 1)) <= t3
    j = lax.broadcasted_iota(jnp.int32, (n_slc, tq), 0)
    cur = (q0 + lax.broadcasted_iota(jnp.int32, (n_slc, tq), 1)) // SLC_BLOCK
    forced = (j == 0) | ((cur - j >= 0) & (cur - j < SLC_LOCAL))
    adm = j <= cur
    outs = []
    for g in range(NSA_KV_HEADS):
        gw = NSA_GROUP * HEAD_DIM
        q4 = _stack_heads(q[:, gw * g:gw * (g + 1)], NSA_GROUP)
        kv = kvc_ref[0, g]
        s = _dot_nt(q4, kv).reshape(NSA_GROUP, tq, n_cmp)
        s = jnp.where(mask, s, NEG_INF)
        m = jnp.max(s, axis=2, keepdims=True)
        e = jnp.where(mask, jnp.exp(s - m), 0.0)
        l = jnp.sum(e, axis=2, keepdims=True)
        p = e * jnp.where(l > 0.0, 1.0 / l, 0.0)
        o = _dot(p.reshape(NSA_GROUP * tq, n_cmp).astype(BF16), kv)
        outs.append(_unstack_heads(o, NSA_GROUP))
        psum = p[0] + p[1] + p[2] + p[3]
        hi = psum.astype(BF16)
        lo = (psum - hi.astype(F32)).astype(BF16)
        p_slc = _dot_nt(ot_ref[...], hi) + _dot_nt(ot_ref[...], lo)
        blk = jnp.where(forced, BIG, p_slc)
        blk = jnp.where(adm, blk, NEG_INF)
        rank = jnp.zeros((n_slc, tq), F32)
        for i in range(n_slc):
            bi = blk[i:i + 1, :]
            beats = (bi > blk) | ((bi == blk) & (j > i))
            rank = rank + jnp.where(beats, 1.0, 0.0)
        selt_ref[0, g] = jnp.where((rank < float(SLC_TOPN)) & adm, 1.0, 0.0)
    ocmp_ref[0] = jnp.concatenate(outs, axis=1)


def _cmp(qa, kvc, ot, tq):
    b, s, _ = qa.shape
    n_cmp = kvc.shape[2]
    n_slc = ot.shape[0]
    return pl.pallas_call(
        _cmp_kernel,
        grid=(b, s // tq),
        in_specs=[pl.BlockSpec((1, tq, NSA_HEADS * HEAD_DIM), lambda i, j: (i, j, 0)),
                  pl.BlockSpec((1, NSA_KV_HEADS, n_cmp, LANES), lambda i, j: (i, 0, 0, 0)),
                  pl.BlockSpec(ot.shape, lambda i, j: (0, 0))],
        out_specs=[pl.BlockSpec((1, tq, NSA_HEADS * HEAD_DIM), lambda i, j: (i, j, 0)),
                   pl.BlockSpec((1, NSA_KV_HEADS, n_slc, tq), lambda i, j: (i, 0, 0, j))],
        out_shape=[jax.ShapeDtypeStruct((b, s, NSA_HEADS * HEAD_DIM), F32),
                   jax.ShapeDtypeStruct((b, NSA_KV_HEADS, n_slc, s), F32)],
        compiler_params=pltpu.CompilerParams(dimension_semantics=("arbitrary", "arbitrary"),
                                             vmem_limit_bytes=VMEM_LIMIT),
    )(qa, kvc, ot)


def _nsa_kernel(qa_ref, kvs_ref, kvw_ref, sel_ref, e_ref, ocmp_ref, misc_ref, o_ref, *, tk):
    tq = qa_ref.shape[1]
    qi = pl.program_id(1)
    q0 = qi * tq
    q = qa_ref[0].astype(F32)
    gsig = _sigmoid(misc_ref[0])
    ocmp = ocmp_ref[0]
    t3 = q0 + lax.broadcasted_iota(jnp.int32, (1, tq, tk), 1)
    lane3 = lax.broadcasted_iota(jnp.int32, (1, tq, tk), 2)
    gw = NSA_GROUP * HEAD_DIM
    lane_g = lax.broadcasted_iota(jnp.int32, (tq, gw), 1)

    def init():
        return (jnp.full((NSA_GROUP, tq, 1), NEG_INF, F32), jnp.zeros((NSA_GROUP, tq, 1), F32),
                jnp.zeros((NSA_GROUP * tq, LANES), F32))

    def gate(branch, g):
        out = None
        for r in range(NSA_GROUP - 1, -1, -1):
            c = _MISC_GA + branch * NSA_HEADS + g * NSA_GROUP + r
            col = jnp.broadcast_to(gsig[:, c:c + 1], (tq, gw))
            out = col if out is None else jnp.where(lane_g < HEAD_DIM * (r + 1), col, out)
        return out

    outs = []
    for g in range(NSA_KV_HEADS):
        q4 = _stack_heads(q[:, gw * g:gw * (g + 1)], NSA_GROUP)
        sel = sel_ref[0, g]

        def slc_body(kt, carry):
            k0 = pl.multiple_of(kt * tk, tk)
            kv = kvs_ref[0, pl.ds(k0, tk), LANES * g:LANES * (g + 1)]
            s = _dot_nt(q4, kv).reshape(NSA_GROUP, tq, tk)
            selm = _dot(sel, e_ref[kt])[None]
            mask = (selm > 0.5) & ((k0 + lane3) <= t3)
            return _softmax_step(s, mask, kv, *carry)

        _, l, acc = lax.fori_loop(0, (q0 + tq + tk - 1) // tk, slc_body, init())
        o_slc = _unstack_heads(_softmax_finish(l, acc), NSA_GROUP)

        def win_body(kt, carry):
            k0 = pl.multiple_of(kt * tk, tk)
            kv = kvw_ref[0, pl.ds(k0, tk), LANES * g:LANES * (g + 1)]
            s = _dot_nt(q4, kv).reshape(NSA_GROUP, tq, tk)
            dist = t3 - (k0 + lane3)
            mask = (dist >= 0) & (dist < WINDOW)
            return _softmax_step(s, mask, kv, *carry)

        kt_lo = jnp.maximum(q0 - (WINDOW - 1), 0) // tk
        _, l, acc = lax.fori_loop(kt_lo, (q0 + tq + tk - 1) // tk, win_body, init())
        o_win = _unstack_heads(_softmax_finish(l, acc), NSA_GROUP)

        outs.append(gate(0, g) * ocmp[:, gw * g:gw * (g + 1)] + gate(1, g) * o_slc + gate(2, g) * o_win)
    o_ref[0] = jnp.concatenate(outs, axis=1).astype(o_ref.dtype)


def _nsa(qa, kvs, kvw, sel, e, ocmp, misc, tq, tk):
    b, s, _ = qa.shape
    n_slc = sel.shape[3]
    hw = NSA_HEADS * HEAD_DIM
    return pl.pallas_call(
        functools.partial(_nsa_kernel, tk=tk),
        grid=(b, s // tq),
        in_specs=[pl.BlockSpec((1, tq, hw), lambda i, j: (i, j, 0)),
                  pl.BlockSpec((1, s, 2 * LANES), lambda i, j: (i, 0, 0)),
                  pl.BlockSpec((1, s, 2 * LANES), lambda i, j: (i, 0, 0)),
                  pl.BlockSpec((1, NSA_KV_HEADS, tq, n_slc), lambda i, j: (i, 0, j, 0)),
                  pl.BlockSpec(e.shape, lambda i, j: (0, 0, 0)),
                  pl.BlockSpec((1, tq, hw), lambda i, j: (i, j, 0)),
                  pl.BlockSpec((1, tq, LANES), lambda i, j: (i, j, 0))],
        out_specs=pl.BlockSpec((1, tq, hw), lambda i, j: (i, j, 0)),
        out_shape=jax.ShapeDtypeStruct((b, s, hw), BF16),
        compiler_params=pltpu.CompilerParams(dimension_semantics=("arbitrary", "arbitrary"),
                                             vmem_limit_bytes=VMEM_LIMIT),
    )(qa, kvs, kvw, sel, e, ocmp, misc)


def _dsa_kernel(qb_ref, kvb_ref, iq_ref, ik_ref, misc_ref, u_ref, o_ref, score_ref, *, topk, n_bisect):
    tq = qb_ref.shape[1]
    tk = score_ref.shape[2]
    qi = pl.program_id(1)
    q0 = qi * tq
    n_kt = (q0 + tq + tk - 1) // tk
    misc = misc_ref[0]
    t_col = q0 + lax.broadcasted_iota(jnp.int32, (tq, 1), 0)
    t2 = q0 + lax.broadcasted_iota(jnp.int32, (tq, tk), 0)
    lane2 = lax.broadcasted_iota(jnp.int32, (tq, tk), 1)

    iq4 = _stack_heads(iq_ref[0].astype(F32), IDX_HEADS)
    wts = [misc[:, _MISC_IW + h:_MISC_IW + h + 1] * (IDX_HEADS ** -0.5) for h in range(IDX_HEADS)]

    def score_body(kt, carry):
        mn, mx = carry
        k0 = pl.multiple_of(kt * tk, tk)
        lg = _dot_nt(iq4, ik_ref[0, pl.ds(k0, tk), :]).reshape(IDX_HEADS, tq, tk)
        sc = wts[0] * jnp.maximum(lg[0], 0.0)
        for h in range(1, IDX_HEADS):
            sc = sc + wts[h] * jnp.maximum(lg[h], 0.0)
        causal = (k0 + lane2) <= t2
        score_ref[kt] = jnp.where(causal, sc, NEG_INF)
        mn = jnp.minimum(mn, jnp.where(causal, sc, _FMAX))
        mx = jnp.maximum(mx, jnp.where(causal, sc, -_FMAX))
        return mn, mx

    mn, mx = lax.fori_loop(0, n_kt, score_body,
                           (jnp.full((tq, tk), _FMAX, F32), jnp.full((tq, tk), -_FMAX, F32)))
    lo = jnp.min(mn, axis=1, keepdims=True)
    hi = jnp.max(mx, axis=1, keepdims=True)
    kf = jnp.minimum(t_col + 1, topk).astype(F32)

    def midpoint(lo, hi):
        mid = lo + (hi - lo) * 0.5
        return jnp.where(mid >= hi, lo, mid)

    def count_gt(thr):
        def body(kt, cnt):
            return cnt + jnp.where(score_ref[kt] > thr, 1.0, 0.0)
        return jnp.sum(lax.fori_loop(0, n_kt, body, jnp.zeros((tq, tk), F32)), axis=1, keepdims=True)

    def bisect(_, carry):
        lo, hi = carry
        mid = midpoint(lo, hi)
        up = count_gt(mid) >= kf
        return jnp.where(up, mid, lo), jnp.where(up, hi, mid)

    lo, hi = lax.fori_loop(0, n_bisect, bisect, (lo, hi))

    def snap_cond(carry):
        lo, hi = carry
        return jnp.max(hi - lo) > 0.0

    def snap(carry):
        lo, hi = carry
        mid = midpoint(lo, hi)

        def body(kt, c):
            cnt, above, below = c
            sc = score_ref[kt]
            gt = sc > mid
            return (cnt + jnp.where(gt, 1.0, 0.0), jnp.minimum(above, jnp.where(gt, sc, _FMAX)),
                    jnp.maximum(below, jnp.where(gt, -_FMAX, sc)))

        cnt, above, below = lax.fori_loop(
            0, n_kt, body, (jnp.zeros((tq, tk), F32), jnp.full((tq, tk), _FMAX, F32),
                            jnp.full((tq, tk), -_FMAX, F32)))
        up = jnp.sum(cnt, axis=1, keepdims=True) >= kf
        return (jnp.where(up, jnp.min(above, axis=1, keepdims=True), lo),
                jnp.where(up, hi, jnp.max(below, axis=1, keepdims=True)))

    v, _ = lax.while_loop(snap_cond, snap, (lo, hi))
    need = kf - count_gt(v)

    q8 = _stack_heads(qb_ref[0].astype(F32), DSA_HEADS)

    def att_body(kt, carry):
        run, m, l, acc = carry
        k0 = pl.multiple_of(kt * tk, tk)
        kv = kvb_ref[0, pl.ds(k0, tk), :]
        sc = score_ref[kt]
        eq = sc == v
        eqf = jnp.where(eq, 1.0, 0.0)
        before = run + _dot(eqf.astype(BF16), u_ref[...])
        take = (sc > v) | (eq & (before < need))
        s = _dot_nt(q8, kv).reshape(DSA_HEADS, tq, tk)
        m, l, acc = _softmax_step(s, take[None], kv, m, l, acc)
        return run + jnp.sum(eqf, axis=1, keepdims=True), m, l, acc

    init = (jnp.zeros((tq, 1), F32), jnp.full((DSA_HEADS, tq, 1), NEG_INF, F32),
            jnp.zeros((DSA_HEADS, tq, 1), F32), jnp.zeros((DSA_HEADS * tq, LANES), F32))
    _, _, l, acc = lax.fori_loop(0, n_kt, att_body, init)
    o_ref[0] = _unstack_heads(_softmax_finish(l, acc), DSA_HEADS).astype(o_ref.dtype)


def _dsa(qb, kvb, iq, ik, misc, u, tq, tk, topk, n_bisect):
    b, s, _ = qb.shape
    hw = DSA_HEADS * HEAD_DIM
    return pl.pallas_call(
        functools.partial(_dsa_kernel, topk=topk, n_bisect=n_bisect),
        grid=(b, s // tq),
        in_specs=[pl.BlockSpec((1, tq, hw), lambda i, j: (i, j, 0)),
                  pl.BlockSpec((1, s, LANES), lambda i, j: (i, 0, 0)),
                  pl.BlockSpec((1, tq, IDX_HEADS * IDX_DIM), lambda i, j: (i, j, 0)),
                  pl.BlockSpec((1, s, LANES), lambda i, j: (i, 0, 0)),
                  pl.BlockSpec((1, tq, LANES), lambda i, j: (i, j, 0)),
                  pl.BlockSpec(u.shape, lambda i, j: (0, 0))],
        out_specs=pl.BlockSpec((1, tq, hw), lambda i, j: (i, j, 0)),
        out_shape=jax.ShapeDtypeStruct((b, s, hw), BF16),
        scratch_shapes=[pltpu.VMEM((s // tk, tq, tk), F32)],
        compiler_params=pltpu.CompilerParams(dimension_semantics=("arbitrary", "arbitrary"),
                                             vmem_limit_bytes=VMEM_LIMIT),
    )(qb, kvb, iq, ik, misc, u)


def _post_kernel(on_ref, od_ref, gm_ref, x_ref, wbn_ref, wbd_ref, wo_ref, gffn_ref, wg_ref, wu_ref, wd_ref,
                 gfin_ref, out_ref):
    ya = _dot(on_ref[...], wbn_ref[...])
    yb = _dot(od_ref[...], wbd_ref[...])
    gm = gm_ref[...]
    merged = _sigmoid(gm[:, :D_MODEL]) * ya + _sigmoid(gm[:, D_MODEL:]) * yb
    h = x_ref[...] + _dot(merged.astype(BF16), wo_ref[...])
    ms = jnp.mean(h * h, axis=-1, keepdims=True)
    hn = (h * lax.rsqrt(ms + NORM_EPS) * gffn_ref[...]).astype(BF16)
    acc = jnp.zeros_like(h)
    for c in range(wg_ref.shape[0]):
        gt = _dot(hn, wg_ref[c])
        up = _dot(hn, wu_ref[c])
        act = (gt * _sigmoid(gt) * up).astype(BF16)
        acc = acc + _dot(act, wd_ref[c])
    h2 = h + acc
    ms2 = jnp.mean(h2 * h2, axis=-1, keepdims=True)
    out_ref[...] = h2 * lax.rsqrt(ms2 + NORM_EPS) * gfin_ref[...]


def _post(o_nsa, o_dsa, gm, x2, wbn, wbd, wo, gffn, wg, wu, wd, gfin, tm):
    n = x2.shape[0]
    row = lambda width: pl.BlockSpec((tm, width), lambda i: (i, 0))
    full = lambda a: pl.BlockSpec(a.shape, lambda i: (0,) * a.ndim)
    return pl.pallas_call(
        _post_kernel,
        grid=(n // tm,),
        in_specs=[row(o_nsa.shape[1]), row(o_dsa.shape[1]), row(2 * D_MODEL), row(D_MODEL),
                  full(wbn), full(wbd), full(wo), full(gffn), full(wg), full(wu), full(wd), full(gfin)],
        out_specs=row(D_MODEL),
        out_shape=jax.ShapeDtypeStruct((n, D_MODEL), F32),
        compiler_params=pltpu.CompilerParams(dimension_semantics=("arbitrary",), vmem_limit_bytes=VMEM_LIMIT),
    )(o_nsa, o_dsa, gm, x2, wbn, wbd, wo, gffn, wg, wu, wd, gfin)


def _layer(h2, b, s, norm_mix, w_in, cmp_pos_k, cmp_w1_k, cmp_w2_k, cmp_pos_v, cmp_w1_v, cmp_w2_v,
           w_branch_nsa, w_branch_dsa, w_out, norm_ffn, w_gate, w_up, w_down, norm_out):
    tq = 128
    n_chunk = s // CMP_STRIDE
    n_slc = s // SLC_BLOCK
    tabs = _rope_tables(s)
    qa, kvc, kvs, kvw, qb, kvb, iq, ik, misc, gm = _proj(
        h2, norm_mix[None, :], _relayout_w_in(w_in), tabs, s, 512)

    c = kvc.reshape(b, n_chunk, CMP_STRIDE, 2, NSA_KV_HEADS, HEAD_DIM).transpose(0, 3, 4, 1, 2, 5)
    c = c.reshape(b, 2, NSA_KV_HEADS, n_chunk, CMP_STRIDE * HEAD_DIM)
    pos = jnp.stack([cmp_pos_k, cmp_pos_v]).reshape(2, 2, 1, CMP_STRIDE * HEAD_DIM)
    w1 = jnp.stack([cmp_w1_k, cmp_w1_v]).astype(BF16)
    zpad = jnp.zeros((CMP_HIDDEN, HEAD_DIM), F32)
    w2 = jnp.stack([jnp.concatenate([cmp_w2_k, zpad], axis=1),
                    jnp.concatenate([zpad, cmp_w2_v], axis=1)]).astype(BF16)
    kvcmp = _compress(c, pos, w1, w2)

    ci = np.arange(n_chunk)[None, :] * CMP_STRIDE
    sj = np.arange(n_slc)[:, None] * SLC_BLOCK
    ot = ((ci < sj + SLC_BLOCK) & (ci + CMP_BLOCK > sj) & (np.arange(n_chunk)[None, :] < n_chunk - 1))
    ocmp, selt = _cmp(qa.reshape(b, s, -1), kvcmp, jnp.asarray(ot, BF16), tq)
    sel = selt.transpose(0, 1, 3, 2).astype(BF16)

    tk = 256
    e = (np.arange(s)[None, :] // SLC_BLOCK == np.arange(n_slc)[:, None])
    e = jnp.asarray(e.reshape(n_slc, s // tk, tk).transpose(1, 0, 2), BF16)
    o_nsa = _nsa(qa.reshape(b, s, -1), kvs.reshape(b, s, -1), kvw.reshape(b, s, -1), sel, e, ocmp,
                 misc.reshape(b, s, -1), tq, tk)

    tkd = 128
    u = jnp.asarray(np.arange(tkd)[:, None] < np.arange(tkd)[None, :], BF16)
    o_dsa = _dsa(qb.reshape(b, s, -1), kvb.reshape(b, s, -1), iq.reshape(b, s, -1), ik.reshape(b, s, -1),
                 misc.reshape(b, s, -1), u, tq, tkd, min(DSA_TOPK_MAX, s // 4), 10)

    fc = 256
    wg = w_gate.astype(BF16).reshape(D_MODEL, D_FF // fc, fc).transpose(1, 0, 2)
    wu = w_up.astype(BF16).reshape(D_MODEL, D_FF // fc, fc).transpose(1, 0, 2)
    wd = w_down.astype(BF16).reshape(D_FF // fc, fc, D_MODEL)
    return _post(o_nsa.reshape(b * s, -1), o_dsa.reshape(b * s, -1), gm, h2,
                 w_branch_nsa.astype(BF16), w_branch_dsa.astype(BF16), w_out.astype(BF16),
                 norm_ffn[None, :], wg, wu, wd, norm_out[None, :], 256)


def kernel(x, norm_mix, w_in, cmp_pos_k, cmp_w1_k, cmp_w2_k, cmp_pos_v, cmp_w1_v, cmp_w2_v, w_branch_nsa,
           w_branch_dsa, w_out, norm_ffn, w_gate, w_up, w_down, norm_final):
    b, s, d = x.shape
    depth = norm_mix.shape[0]
    assert depth == 1, "the fused epilogue applies the final norm right after the single layer"
    out = _layer(x.reshape(b * s, d), b, s, norm_mix[0], w_in[0], cmp_pos_k[0], cmp_w1_k[0], cmp_w2_k[0],
                 cmp_pos_v[0], cmp_w1_v[0], cmp_w2_v[0], w_branch_nsa[0], w_branch_dsa[0], w_out[0],
                 norm_ffn[0], w_gate[0], w_up[0], w_down[0], norm_final)
    return out.reshape(b, s, d)
```

```python
import functools
import math

import numpy as np
import jax
import jax.numpy as jnp
from jax import lax
from jax.experimental import pallas as pl
from jax.experimental.pallas import tpu as pltpu

F32 = jnp.float32
BF16 = jnp.bfloat16

D_MODEL = 1024
HEAD_DIM = 64
ROPE_THETA = 10000.0
NORM_EPS = 1e-6
NEG_INF = -1e30
BIG = 1e4
_FMAX = 3.0e38
NSA_HEADS = 8
NSA_KV_HEADS = 2
NSA_GROUP = NSA_HEADS // NSA_KV_HEADS
CMP_BLOCK = 32
CMP_STRIDE = 16
CMP_HIDDEN = 256
SLC_BLOCK = 64
SLC_TOPN = 8
SLC_LOCAL = 2
WINDOW = 512
DSA_HEADS = 8
IDX_HEADS = 4
IDX_DIM = 64
DSA_TOPK_MAX = 256
D_FF = -(-8 * D_MODEL // (3 * 256)) * 256

LANES = 128
SUBLANES = 8
VMEM_LIMIT = 56 * 1024 * 1024

_O_QA, _O_KC, _O_VC, _O_KSL, _O_VSL, _O_KWN, _O_VWN = 0, 512, 640, 768, 896, 1024, 1152
_O_GA, _O_QB, _O_KB, _O_VB, _O_IQ, _O_IK, _O_IW, _O_GM = 1280, 1304, 1816, 1880, 1944, 2200, 2264, 2268
_D_IN = 4316
_N_QA, _N_KVC, _N_KVS, _N_KVW, _N_QB, _N_KVB, _N_IQ, _N_IK, _N_MISC, _N_GM = (
    0, 512, 768, 1024, 1280, 1792, 1920, 2176, 2304, 2432)
_D_IN_PAD = 4480
_MISC_GA, _MISC_IW = 0, 24


def _dot(a, b):
    return jnp.dot(a, b, preferred_element_type=F32)


def _dot_nt(a, b):
    return lax.dot_general(a, b, (((1,), (1,)), ((), ())), preferred_element_type=F32)


def _sigmoid(x):
    return 1.0 / (1.0 + jnp.exp(-x))


def _stack_heads(x, n_heads):
    tq = x.shape[0]
    low = lax.broadcasted_iota(jnp.int32, (tq, LANES), 1) < HEAD_DIM
    parts = []
    for p in range(n_heads // 2):
        slab = x[:, LANES * p:LANES * (p + 1)]
        parts.append(jnp.where(low, slab, 0.0))
        parts.append(jnp.where(low, pltpu.roll(slab, HEAD_DIM, 1), 0.0))
    return jnp.concatenate(parts, axis=0).astype(BF16)


def _unstack_heads(acc, n_heads):
    tq = acc.shape[0] // n_heads
    low = lax.broadcasted_iota(jnp.int32, (tq, LANES), 1) < HEAD_DIM
    outs = []
    for p in range(n_heads // 2):
        a0 = acc[(2 * p) * tq:(2 * p + 1) * tq]
        a1 = acc[(2 * p + 1) * tq:(2 * p + 2) * tq]
        outs.append(jnp.where(low, pltpu.roll(a0, HEAD_DIM, 1), a1))
    return jnp.concatenate(outs, axis=1)


def _flash_init(m_ref, acc_ref):
    m_ref[...] = jnp.full(m_ref.shape, NEG_INF, F32)
    acc_ref[...] = jnp.zeros(acc_ref.shape, F32)


def _flash_step(q_all, kvs, ovs, biases, m_ref, acc_ref):
    n_heads, tq, _ = m_ref.shape
    per_group = n_heads // len(kvs)
    tk = kvs[0].shape[0]
    m_old = [m_ref[h] for h in range(n_heads)]
    acc_old = [acc_ref[h] for h in range(n_heads)]
    m_new, alpha, pv = [], [], []
    for j, (kv, ov, bias) in enumerate(zip(kvs, ovs, biases)):
        rows = slice(j * per_group * tq, (j + 1) * per_group * tq)
        s_all = _dot_nt(q_all[rows], kv)
        p = []
        for r in range(per_group):
            h = j * per_group + r
            s = s_all[r * tq:(r + 1) * tq] + bias
            m_h = jnp.maximum(m_old[h], jnp.max(s, axis=1, keepdims=True))
            m_wide = m_h if tk == LANES else jnp.concatenate([m_h] * (tk // LANES), axis=1)
            p.append(jnp.exp2(s - m_wide).astype(BF16))
            alpha.append(jnp.exp2(m_old[h] - m_h))
            m_new.append(m_h)
        pv_all = _dot(jnp.concatenate(p, axis=0), ov)
        pv.extend(pv_all[r * tq:(r + 1) * tq] for r in range(per_group))
    for h in range(n_heads):
        acc_ref[h] = alpha[h] * acc_old[h] + pv[h]
        m_ref[h] = m_new[h]


def _flash_finish(acc_ref, n_heads):
    tq = acc_ref.shape[1]
    low = lax.broadcasted_iota(jnp.int32, (tq, LANES), 1) < HEAD_DIM
    outs = []
    for h in range(n_heads):
        acc = acc_ref[h]
        inv = 1.0 / jnp.where(low, acc, 1.0)
        outs.append(acc * pltpu.roll(inv, HEAD_DIM, 1))
    return _unstack_heads(jnp.concatenate(outs, axis=0), n_heads)


_ROPE_Q, _ROPE_IQ, _ROPE_K, _ROPE_KV = 0, 1, 2, 3


def _proj_kernel(x_ref, g_ref, w_ref, tab_ref, qa_ref, kvc_ref, kvs_ref, ovs_ref, kvw_ref, ovw_ref, qb_ref,
                 kvb_ref, ovb_ref, iq_ref, ik_ref, misc_ref, gm_ref):
    x = x_ref[...]
    tm = x.shape[0]
    ms = jnp.mean(x * x, axis=-1, keepdims=True)
    xn = (x * lax.rsqrt(ms + NORM_EPS) * g_ref[...]).astype(BF16)
    lane = lax.broadcasted_iota(jnp.int32, (tm, LANES), 1)
    low32 = (lane & (HEAD_DIM // 2)) == 0
    low64 = lane < HEAD_DIM

    def rope(z, kind):
        cos = tab_ref[2 * kind]
        sin = tab_ref[2 * kind + 1]
        rot = jnp.where(low32, pltpu.roll(z, LANES - HEAD_DIM // 2, 1), pltpu.roll(z, HEAD_DIM // 2, 1))
        return z * cos + rot * sin

    def emit(col, width, out_ref, out_col, kinds, ones_ref=None):
        z = _dot(xn, w_ref[:, col:col + width])
        for j in range(width // LANES):
            zj = z[:, LANES * j:LANES * (j + 1)]
            if kinds[j] is not None:
                zj = rope(zj, kinds[j])
            cols = slice(out_col + LANES * j, out_col + LANES * (j + 1))
            out_ref[:, cols] = zj.astype(out_ref.dtype)
            if ones_ref is not None:
                ones_ref[:, cols] = jnp.where(low64, 1.0, zj).astype(ones_ref.dtype)

    emit(_N_QA, 512, qa_ref, 0, [_ROPE_Q] * 4)
    emit(_N_KVC, 256, kvc_ref, 0, [_ROPE_K, None])
    emit(_N_KVS, 256, kvs_ref, 0, [_ROPE_KV] * 2, ovs_ref)
    emit(_N_KVW, 256, kvw_ref, 0, [_ROPE_KV] * 2, ovw_ref)
    emit(_N_QB, 512, qb_ref, 0, [_ROPE_Q] * 4)
    emit(_N_KVB, 128, kvb_ref, 0, [_ROPE_KV], ovb_ref)
    emit(_N_IQ, 256, iq_ref, 0, [_ROPE_IQ] * 2)
    emit(_N_IK, 128, ik_ref, 0, [_ROPE_K])
    emit(_N_MISC, 128, misc_ref, 0, [None])
    for c in range(4):
        emit(_N_GM + 512 * c, 512, gm_ref, 512 * c, [None] * 4)


def _relayout_w_in(w):
    z = lambda n: jnp.zeros((w.shape[0], n), w.dtype)
    s = lambda a, n: w[:, a:a + n]
    cols = [
        s(_O_QA, 512), s(_O_KC, 128), s(_O_VC, 128),
        s(_O_KSL, 64), s(_O_VSL, 64), s(_O_KSL + 64, 64), s(_O_VSL + 64, 64),
        s(_O_KWN, 64), s(_O_VWN, 64), s(_O_KWN + 64, 64), s(_O_VWN + 64, 64),
        s(_O_QB, 512), s(_O_KB, 64), s(_O_VB, 64), s(_O_IQ, 256),
        s(_O_IK, 64), z(64),
        s(_O_GA, 24), s(_O_IW, 4), z(100),
        s(_O_GM, 2048),
    ]
    out = jnp.concatenate(cols, axis=1)
    assert out.shape[1] == _D_IN_PAD
    return out.astype(BF16)


def _rope_tables(seq):
    half = HEAD_DIM // 2
    inv_freq = ROPE_THETA ** (-jnp.arange(half, dtype=F32) / half)
    ang = jnp.arange(seq, dtype=F32)[:, None] * inv_freq[None, :]
    cos, sin = jnp.cos(ang), jnp.sin(ang)
    cos64 = jnp.concatenate([cos, cos], axis=1)
    sin64 = jnp.concatenate([-sin, sin], axis=1)
    cos_k = jnp.concatenate([cos64, cos64], axis=1)
    sin_k = jnp.concatenate([sin64, sin64], axis=1)
    scale = HEAD_DIM ** -0.5
    q_scale = scale * math.log2(math.e)
    cos_kv = jnp.concatenate([cos64, jnp.ones_like(cos64)], axis=1)
    sin_kv = jnp.concatenate([sin64, jnp.zeros_like(sin64)], axis=1)
    return jnp.stack([cos_k * q_scale, sin_k * q_scale, cos_k * scale, sin_k * scale,
                      cos_k, sin_k, cos_kv, sin_kv], axis=0)


def _proj(x2, gain, w_pad, tabs, seq, tm):
    n = x2.shape[0]
    nblk_seq = seq // tm
    row = lambda width: pl.BlockSpec((tm, width), lambda i: (i, 0))
    full = lambda shape: pl.BlockSpec(shape, lambda i: (0,) * len(shape))
    widths = [(512, BF16), (256, F32), (256, BF16), (256, BF16), (256, BF16), (256, BF16), (512, BF16),
              (128, BF16), (128, BF16), (256, BF16), (128, BF16), (128, F32), (2048, F32)]
    return pl.pallas_call(
        _proj_kernel,
        grid=(n // tm,),
        in_specs=[row(D_MODEL), full((1, D_MODEL)), full((D_MODEL, _D_IN_PAD)),
                  pl.BlockSpec((tabs.shape[0], tm, LANES), lambda i: (0, i % nblk_seq, 0))],
        out_specs=[row(wd) for wd, _ in widths],
        out_shape=[jax.ShapeDtypeStruct((n, wd), dt) for wd, dt in widths],
        compiler_params=pltpu.CompilerParams(dimension_semantics=("arbitrary",), vmem_limit_bytes=VMEM_LIMIT),
    )(x2, gain, w_pad, tabs)


def _compress_kernel(c_ref, pos_ref, w1_ref, w2_ref, out_ref):
    n_chunk = c_ref.shape[3]
    for h in range(NSA_KV_HEADS):
        acc = jnp.zeros((n_chunk, LANES), F32)
        for kv in range(2):
            c = c_ref[0, kv, h]
            a_lo = (c + pos_ref[kv, 0]).astype(BF16)
            a_hi = (c + pos_ref[kv, 1]).astype(BF16)
            half = CMP_STRIDE * HEAD_DIM
            h_lo = _dot(a_lo, w1_ref[kv, :half, :])
            h_hi = _dot(a_hi, w1_ref[kv, half:, :])
            hid = h_lo + pltpu.roll(h_hi, n_chunk - 1, 0)
            act = jax.nn.gelu(hid, approximate=True).astype(BF16)
            acc = acc + _dot(act, w2_ref[kv])
        out_ref[0, h] = acc.astype(out_ref.dtype)


def _compress(c, pos, w1, w2):
    b = c.shape[0]
    n_chunk = c.shape[3]
    return pl.pallas_call(
        _compress_kernel,
        grid=(b,),
        in_specs=[pl.BlockSpec((1,) + c.shape[1:], lambda i: (i, 0, 0, 0, 0)),
                  pl.BlockSpec(pos.shape, lambda i: (0, 0, 0, 0)),
                  pl.BlockSpec(w1.shape, lambda i: (0, 0, 0)),
                  pl.BlockSpec(w2.shape, lambda i: (0, 0, 0))],
        out_specs=pl.BlockSpec((1, NSA_KV_HEADS, n_chunk, LANES), lambda i: (i, 0, 0, 0)),
        out_shape=jax.ShapeDtypeStruct((b, NSA_KV_HEADS, n_chunk, LANES), BF16),
        compiler_params=pltpu.CompilerParams(dimension_semantics=("arbitrary",), vmem_limit_bytes=VMEM_LIMIT),
    )(c, pos, w1, w2)


def _cmp_kernel(qa_ref, kvc_ref, ot_ref, ocmp_ref, selt_ref):
    tq = qa_ref.shape[1]
    n_cmp = kvc_ref.shape[2]
    n_slc = ot_ref.shape[0]
    q0 = pl.program_id(1) * tq
    q = qa_ref[0].astype(F32)
    t3 = q0 + lax.broadcasted_iota(jnp.int32, (NSA_GROUP, tq, n_cmp), 1)
    n3 = lax.broadcasted_iota(jnp.int32, (NSA_GROUP, tq, n_cmp), 2)
    mask = (n3 * CMP_STRIDE + (CMP_BLOCK - 1)) <= t3
    j = lax.broadcasted_iota(jnp.int32, (n_slc, tq), 0)
    cur = (q0 + lax.broadcasted_iota(jnp.int32, (n_slc, tq), 1)) // SLC_BLOCK
    forced = (j == 0) | ((cur - j >= 0) & (cur - j < SLC_LOCAL))
    adm = j <= cur
    outs = []
    for g in range(NSA_KV_HEADS):
        gw = NSA_GROUP * HEAD_DIM
        q4 = _stack_heads(q[:, gw * g:gw * (g + 1)], NSA_GROUP)
        kv = kvc_ref[0, g]
        s = _dot_nt(q4, kv).reshape(NSA_GROUP, tq, n_cmp)
        s = jnp.where(mask, s, NEG_INF)
        m = jnp.max(s, axis=2, keepdims=True)
        e = jnp.where(mask, jnp.exp2(s - m), 0.0)
        l = jnp.sum(e, axis=2, keepdims=True)
        p = e * jnp.where(l > 0.0, 1.0 / l, 0.0)
        o = _dot(p.reshape(NSA_GROUP * tq, n_cmp).astype(BF16), kv)
        outs.append(_unstack_heads(o, NSA_GROUP))
        psum = p[0] + p[1] + p[2] + p[3]
        hi = psum.astype(BF16)
        lo = (psum - hi.astype(F32)).astype(BF16)
        p_slc = _dot_nt(ot_ref[...], hi) + _dot_nt(ot_ref[...], lo)
        blk = jnp.where(forced, BIG, p_slc)
        blk = jnp.where(adm, blk, NEG_INF)
        rank = jnp.zeros((n_slc, tq), F32)
        for i in range(n_slc):
            bi = blk[i:i + 1, :]
            beats = (bi > blk) | ((bi == blk) & (j > i))
            rank = rank + jnp.where(beats, 1.0, 0.0)
        selt_ref[0, g] = jnp.where((rank < float(SLC_TOPN)) & adm, 1.0, 0.0)
    ocmp_ref[0] = jnp.concatenate(outs, axis=1)


def _cmp(qa, kvc, ot, tq):
    b, s, _ = qa.shape
    n_cmp = kvc.shape[2]
    n_slc = ot.shape[0]
    return pl.pallas_call(
        _cmp_kernel,
        grid=(b, s // tq),
        in_specs=[pl.BlockSpec((1, tq, NSA_HEADS * HEAD_DIM), lambda i, j: (i, j, 0)),
                  pl.BlockSpec((1, NSA_KV_HEADS, n_cmp, LANES), lambda i, j: (i, 0, 0, 0)),
                  pl.BlockSpec(ot.shape, lambda i, j: (0, 0))],
        out_specs=[pl.BlockSpec((1, tq, NSA_HEADS * HEAD_DIM), lambda i, j: (i, j, 0)),
                   pl.BlockSpec((1, NSA_KV_HEADS, n_slc, tq), lambda i, j: (i, 0, 0, j))],
        out_shape=[jax.ShapeDtypeStruct((b, s, NSA_HEADS * HEAD_DIM), F32),
                   jax.ShapeDtypeStruct((b, NSA_KV_HEADS, n_slc, s), F32)],
        compiler_params=pltpu.CompilerParams(dimension_semantics=("arbitrary", "arbitrary"),
                                             vmem_limit_bytes=VMEM_LIMIT),
    )(qa, kvc, ot)


def _nsa_kernel(qa_ref, kvs_ref, ovs_ref, kvw_ref, ovw_ref, sel_ref, e_ref, ocmp_ref, misc_ref, o_ref,
                q_ref, m_ref, acc_ref, *, tk):
    tq = qa_ref.shape[1]
    qi = pl.program_id(1)
    q0 = qi * tq
    span = WINDOW + tq
    gsig = _sigmoid(misc_ref[0])
    ocmp = ocmp_ref[0]
    t_s = q0 + lax.broadcasted_iota(jnp.int32, (tq, tk), 0)
    lane_s = lax.broadcasted_iota(jnp.int32, (tq, tk), 1)
    hw = NSA_HEADS * HEAD_DIM
    lane_o = lax.broadcasted_iota(jnp.int32, (tq, hw), 1)
    groups = range(NSA_KV_HEADS)
    cols = [slice(LANES * g, LANES * (g + 1)) for g in groups]

    def gate(branch):
        out = None
        for h in range(NSA_HEADS - 1, -1, -1):
            c = _MISC_GA + branch * NSA_HEADS + h
            col = jnp.broadcast_to(gsig[:, c:c + 1], (tq, hw))
            out = col if out is None else jnp.where(lane_o < HEAD_DIM * (h + 1), col, out)
        return out

    q_ref[...] = _stack_heads(qa_ref[0].astype(F32), NSA_HEADS)

    def slc_body(kt, carry):
        k0 = pl.multiple_of(kt * tk, tk)
        causal = (k0 + lane_s) <= t_s
        biases = [jnp.where((_dot(sel_ref[0, g], e_ref[kt]) > 0.5) & causal, 0.0, NEG_INF) for g in groups]
        _flash_step(q_ref[...], [kvs_ref[0, pl.ds(k0, tk), cols[g]] for g in groups],
                    [ovs_ref[0, pl.ds(k0, tk), cols[g]] for g in groups], biases, m_ref, acc_ref)
        return carry

    _flash_init(m_ref, acc_ref)
    lax.fori_loop(0, (q0 + tq + tk - 1) // tk, slc_body, 0)
    o_slc = _flash_finish(acc_ref, NSA_HEADS)

    k_lo = pl.multiple_of(jnp.maximum(q0 + tq - span, 0), tq)
    dist = (q0 + lax.broadcasted_iota(jnp.int32, (tq, span), 0)) - (k_lo + lax.broadcasted_iota(
        jnp.int32, (tq, span), 1))
    bias_w = jnp.where((dist >= 0) & (dist < WINDOW), 0.0, NEG_INF)
    _flash_init(m_ref, acc_ref)
    _flash_step(q_ref[...], [kvw_ref[0, pl.ds(k_lo, span), cols[g]] for g in groups],
                [ovw_ref[0, pl.ds(k_lo, span), cols[g]] for g in groups], [bias_w] * NSA_KV_HEADS, m_ref, acc_ref)
    o_win = _flash_finish(acc_ref, NSA_HEADS)

    o_ref[0] = (gate(0) * ocmp + gate(1) * o_slc + gate(2) * o_win).astype(o_ref.dtype)


def _nsa(qa, kvs, ovs, kvw, ovw, sel, e, ocmp, misc, tq, tk):
    b, s, _ = qa.shape
    n_slc = sel.shape[3]
    hw = NSA_HEADS * HEAD_DIM
    assert WINDOW % tq == 0 and WINDOW + tq <= s
    seq = lambda: pl.BlockSpec((1, s, 2 * LANES), lambda i, j: (i, 0, 0))
    return pl.pallas_call(
        functools.partial(_nsa_kernel, tk=tk),
        grid=(b, s // tq),
        in_specs=[pl.BlockSpec((1, tq, hw), lambda i, j: (i, j, 0)),
                  seq(), seq(), seq(), seq(),
                  pl.BlockSpec((1, NSA_KV_HEADS, tq, n_slc), lambda i, j: (i, 0, j, 0)),
                  pl.BlockSpec(e.shape, lambda i, j: (0, 0, 0)),
                  pl.BlockSpec((1, tq, hw), lambda i, j: (i, j, 0)),
                  pl.BlockSpec((1, tq, LANES), lambda i, j: (i, j, 0))],
        out_specs=pl.BlockSpec((1, tq, hw), lambda i, j: (i, j, 0)),
        out_shape=jax.ShapeDtypeStruct((b, s, hw), BF16),
        scratch_shapes=[pltpu.VMEM((NSA_HEADS * tq, LANES), BF16),
                        pltpu.VMEM((NSA_HEADS, tq, LANES), F32),
                        pltpu.VMEM((NSA_HEADS, tq, LANES), F32)],
        compiler_params=pltpu.CompilerParams(dimension_semantics=("arbitrary", "arbitrary"),
                                             vmem_limit_bytes=VMEM_LIMIT),
    )(qa, kvs, ovs, kvw, ovw, sel, e, ocmp, misc)


def _dsa_kernel(qb_ref, kvb_ref, ovb_ref, iq_ref, ik_ref, misc_ref, l2_ref, o_ref,
                score_ref, q_ref, m_ref, acc_ref, *, topk, n_bisect):
    tq = qb_ref.shape[1]
    ts = score_ref.shape[1]
    ta = 2 * ts
    assert ts == tq == LANES
    fold = ts // SUBLANES
    qi = pl.program_id(1)
    q0 = qi * tq
    n_pair = (qi + 2) // 2
    key = lax.broadcasted_iota(jnp.int32, (ts, tq), 0)
    qry = lax.broadcasted_iota(jnp.int32, (ts, tq), 1)

    def fold_min(x):
        return jnp.min(x.reshape(fold, SUBLANES, tq), axis=0)

    def fold_max(x):
        return jnp.max(x.reshape(fold, SUBLANES, tq), axis=0)

    def fold_sum(x):
        return jnp.sum(x.reshape(fold, SUBLANES, tq), axis=0)

    q_ref[0:IDX_HEADS * tq] = _stack_heads(iq_ref[0].astype(F32), IDX_HEADS)
    misc_t = misc_ref[0].T
    wts = [misc_t[_MISC_IW + h:_MISC_IW + h + 1, :] * (IDX_HEADS ** -0.5) for h in range(IDX_HEADS)]

    def score_body(c, carry):
        mn, mx = carry
        k0 = pl.multiple_of(c * ta, ta)
        lg = _dot_nt(ik_ref[0, pl.ds(k0, ta), :], q_ref[0:IDX_HEADS * tq])
        for j in range(2):
            rows = slice(j * ts, (j + 1) * ts)
            sc = wts[0] * jnp.maximum(lg[rows, 0:tq], 0.0)
            for h in range(1, IDX_HEADS):
                sc = sc + wts[h] * jnp.maximum(lg[rows, h * tq:(h + 1) * tq], 0.0)
            causal = (k0 + j * ts + key) <= (q0 + qry)
            score_ref[2 * c + j] = jnp.where(causal, sc, NEG_INF)
            mn = jnp.minimum(mn, fold_min(jnp.where(causal, sc, _FMAX)))
            mx = jnp.maximum(mx, fold_max(jnp.where(causal, sc, -_FMAX)))
        return mn, mx

    mn, mx = lax.fori_loop(0, n_pair, score_body, (jnp.full((SUBLANES, tq), _FMAX, F32),
                                                   jnp.full((SUBLANES, tq), -_FMAX, F32)))
    lo = jnp.min(mn, axis=0, keepdims=True)
    hi = jnp.max(mx, axis=0, keepdims=True)
    kf = jnp.minimum(q0 + lax.broadcasted_iota(jnp.int32, (1, tq), 1) + 1, topk).astype(F32)

    def midpoint(lo, hi):
        mid = lo + (hi - lo) * 0.5
        return jnp.where(mid >= hi, lo, mid)

    def count_gt(thr):
        def body(c, cnt):
            for j in range(2):
                cnt = cnt + fold_sum(jnp.where(score_ref[2 * c + j] > thr, 1.0, 0.0))
            return cnt
        cnt = lax.fori_loop(0, n_pair, body, jnp.zeros((SUBLANES, tq), F32))
        return jnp.sum(cnt, axis=0, keepdims=True)

    def bisect(_, carry):
        lo, hi = carry
        mid = midpoint(lo, hi)
        up = count_gt(mid) >= kf
        return jnp.where(up, mid, lo), jnp.where(up, hi, mid)

    lo, hi = lax.fori_loop(0, n_bisect, bisect, (lo, hi))

    def snap_cond(carry):
        lo, hi = carry
        return jnp.max(hi - lo) > 0.0

    def snap(carry):
        lo, hi = carry
        mid = midpoint(lo, hi)

        def body(c, carry):
            cnt, above, below = carry
            for j in range(2):
                sc = score_ref[2 * c + j]
                gt = sc > mid
                cnt = cnt + fold_sum(jnp.where(gt, 1.0, 0.0))
                above = jnp.minimum(above, fold_min(jnp.where(gt, sc, _FMAX)))
                below = jnp.maximum(below, fold_max(jnp.where(gt, -_FMAX, sc)))
            return cnt, above, below

        cnt, above, below = lax.fori_loop(
            0, n_pair, body, (jnp.zeros((SUBLANES, tq), F32), jnp.full((SUBLANES, tq), _FMAX, F32),
                              jnp.full((SUBLANES, tq), -_FMAX, F32)))
        up = jnp.sum(cnt, axis=0, keepdims=True) >= kf
        return (jnp.where(up, jnp.min(above, axis=0, keepdims=True), lo),
                jnp.where(up, hi, jnp.max(below, axis=0, keepdims=True)))

    v, _ = lax.while_loop(snap_cond, lambda carry: snap(snap(carry)), (lo, hi))
    need = kf - count_gt(v)

    q_ref[...] = _stack_heads(qb_ref[0].astype(F32), DSA_HEADS)
    _flash_init(m_ref, acc_ref)

    def att_body(c, run):
        k0 = pl.multiple_of(c * ta, ta)
        halves = []
        for j in range(2):
            sc = score_ref[2 * c + j]
            eq = sc == v
            pt = _dot(l2_ref[...], jnp.where(eq, 1.0, 0.0).astype(BF16))
            take = (sc > v) | (eq & (run + pt[:ts] < need))
            halves.append(jnp.where(take, 0.0, NEG_INF).T)
            run = run + pt[ts:ts + 1]
        _flash_step(q_ref[...], [kvb_ref[0, pl.ds(k0, ta), :]], [ovb_ref[0, pl.ds(k0, ta), :]],
                    [jnp.concatenate(halves, axis=1)], m_ref, acc_ref)
        return run

    lax.fori_loop(0, n_pair, att_body, jnp.zeros((1, tq), F32))
    o_ref[0] = _flash_finish(acc_ref, DSA_HEADS).astype(o_ref.dtype)


def _dsa(qb, kvb, ovb, iq, ik, misc, l2, tq, topk, n_bisect):
    b, s, _ = qb.shape
    hw = DSA_HEADS * HEAD_DIM
    assert (s // tq) % 2 == 0
    return pl.pallas_call(
        functools.partial(_dsa_kernel, topk=topk, n_bisect=n_bisect),
        grid=(b, s // tq),
        in_specs=[pl.BlockSpec((1, tq, hw), lambda i, j: (i, j, 0)),
                  pl.BlockSpec((1, s, LANES), lambda i, j: (i, 0, 0)),
                  pl.BlockSpec((1, s, LANES), lambda i, j: (i, 0, 0)),
                  pl.BlockSpec((1, tq, IDX_HEADS * IDX_DIM), lambda i, j: (i, j, 0)),
                  pl.BlockSpec((1, s, LANES), lambda i, j: (i, 0, 0)),
                  pl.BlockSpec((1, tq, LANES), lambda i, j: (i, j, 0)),
                  pl.BlockSpec(l2.shape, lambda i, j: (0, 0))],
        out_specs=pl.BlockSpec((1, tq, hw), lambda i, j: (i, j, 0)),
        out_shape=jax.ShapeDtypeStruct((b, s, hw), BF16),
        scratch_shapes=[pltpu.VMEM((s // tq, tq, tq), F32),
                        pltpu.VMEM((DSA_HEADS * tq, LANES), BF16),
                        pltpu.VMEM((DSA_HEADS, tq, LANES), F32),
                        pltpu.VMEM((DSA_HEADS, tq, LANES), F32)],
        compiler_params=pltpu.CompilerParams(dimension_semantics=("arbitrary", "arbitrary"),
                                             vmem_limit_bytes=VMEM_LIMIT),
    )(qb, kvb, ovb, iq, ik, misc, l2)


def _post_kernel(on_ref, od_ref, gm_ref, x_ref, wbn_ref, wbd_ref, wo_ref, gffn_ref, wg_ref, wu_ref, wd_ref,
                 gfin_ref, out_ref):
    ya = _dot(on_ref[...], wbn_ref[...])
    yb = _dot(od_ref[...], wbd_ref[...])
    gm = gm_ref[...]
    merged = _sigmoid(gm[:, :D_MODEL]) * ya + _sigmoid(gm[:, D_MODEL:]) * yb
    h = x_ref[...] + _dot(merged.astype(BF16), wo_ref[...])
    ms = jnp.mean(h * h, axis=-1, keepdims=True)
    hn = (h * lax.rsqrt(ms + NORM_EPS) * gffn_ref[...]).astype(BF16)
    acc = jnp.zeros_like(h)
    for c in range(wg_ref.shape[0]):
        gt = _dot(hn, wg_ref[c])
        up = _dot(hn, wu_ref[c])
        act = (gt * _sigmoid(gt) * up).astype(BF16)
        acc = acc + _dot(act, wd_ref[c])
    h2 = h + acc
    ms2 = jnp.mean(h2 * h2, axis=-1, keepdims=True)
    out_ref[...] = h2 * lax.rsqrt(ms2 + NORM_EPS) * gfin_ref[...]


def _post(o_nsa, o_dsa, gm, x2, wbn, wbd, wo, gffn, wg, wu, wd, gfin, tm):
    n = x2.shape[0]
    row = lambda width: pl.BlockSpec((tm, width), lambda i: (i, 0))
    full = lambda a: pl.BlockSpec(a.shape, lambda i: (0,) * a.ndim)
    return pl.pallas_call(
        _post_kernel,
        grid=(n // tm,),
        in_specs=[row(o_nsa.shape[1]), row(o_dsa.shape[1]), row(2 * D_MODEL), row(D_MODEL),
                  full(wbn), full(wbd), full(wo), full(gffn), full(wg), full(wu), full(wd), full(gfin)],
        out_specs=row(D_MODEL),
        out_shape=jax.ShapeDtypeStruct((n, D_MODEL), F32),
        compiler_params=pltpu.CompilerParams(dimension_semantics=("arbitrary",), vmem_limit_bytes=VMEM_LIMIT),
    )(o_nsa, o_dsa, gm, x2, wbn, wbd, wo, gffn, wg, wu, wd, gfin)


def _layer(h2, b, s, norm_mix, w_in, cmp_pos_k, cmp_w1_k, cmp_w2_k, cmp_pos_v, cmp_w1_v, cmp_w2_v,
           w_branch_nsa, w_branch_dsa, w_out, norm_ffn, w_gate, w_up, w_down, norm_out):
    tq = LANES
    n_chunk = s // CMP_STRIDE
    n_slc = s // SLC_BLOCK
    tabs = _rope_tables(s)
    qa, kvc, kvs, ovs, kvw, ovw, qb, kvb, ovb, iq, ik, misc, gm = _proj(
        h2, norm_mix[None, :], _relayout_w_in(w_in), tabs, s, 512)
    seq3 = lambda a: a.reshape(b, s, -1)

    c = kvc.reshape(b, n_chunk, CMP_STRIDE, 2, NSA_KV_HEADS, HEAD_DIM).transpose(0, 3, 4, 1, 2, 5)
    c = c.reshape(b, 2, NSA_KV_HEADS, n_chunk, CMP_STRIDE * HEAD_DIM)
    pos = jnp.stack([cmp_pos_k, cmp_pos_v]).reshape(2, 2, 1, CMP_STRIDE * HEAD_DIM)
    w1 = jnp.stack([cmp_w1_k, cmp_w1_v]).astype(BF16)
    zpad = jnp.zeros((CMP_HIDDEN, HEAD_DIM), F32)
    w2 = jnp.stack([jnp.concatenate([cmp_w2_k, zpad], axis=1),
                    jnp.concatenate([zpad, cmp_w2_v], axis=1)]).astype(BF16)
    kvcmp = _compress(c, pos, w1, w2)

    ci = np.arange(n_chunk)[None, :] * CMP_STRIDE
    sj = np.arange(n_slc)[:, None] * SLC_BLOCK
    ot = ((ci < sj + SLC_BLOCK) & (ci + CMP_BLOCK > sj) & (np.arange(n_chunk)[None, :] < n_chunk - 1))
    ocmp, selt = _cmp(seq3(qa), kvcmp, jnp.asarray(ot, BF16), tq)
    sel = selt.transpose(0, 1, 3, 2).astype(BF16)

    tk = 256
    e = (np.arange(s)[None, :] // SLC_BLOCK == np.arange(n_slc)[:, None])
    e = jnp.asarray(e.reshape(n_slc, s // tk, tk).transpose(1, 0, 2), BF16)
    o_nsa = _nsa(seq3(qa), seq3(kvs), seq3(ovs), seq3(kvw), seq3(ovw), sel, e, ocmp, seq3(misc), tq, tk)

    idx = np.arange(tq)
    l2 = jnp.asarray(np.concatenate([idx[:, None] > idx[None, :], np.ones((tq, tq), bool)], axis=0), BF16)
    o_dsa = _dsa(seq3(qb), seq3(kvb), seq3(ovb), seq3(iq), seq3(ik), seq3(misc), l2, tq,
                 min(DSA_TOPK_MAX, s // 4), 12)

    fc = 256
    wg = w_gate.astype(BF16).reshape(D_MODEL, D_FF // fc, fc).transpose(1, 0, 2)
    wu = w_up.astype(BF16).reshape(D_MODEL, D_FF // fc, fc).transpose(1, 0, 2)
    wd = w_down.astype(BF16).reshape(D_FF // fc, fc, D_MODEL)
    return _post(o_nsa.reshape(b * s, -1), o_dsa.reshape(b * s, -1), gm, h2,
                 w_branch_nsa.astype(BF16), w_branch_dsa.astype(BF16), w_out.astype(BF16),
                 norm_ffn[None, :], wg, wu, wd, norm_out[None, :], 256)


def kernel(x, norm_mix, w_in, cmp_pos_k, cmp_w1_k, cmp_w2_k, cmp_pos_v, cmp_w1_v, cmp_w2_v, w_branch_nsa,
           w_branch_dsa, w_out, norm_ffn, w_gate, w_up, w_down, norm_final):
    b, s, d = x.shape
    depth = norm_mix.shape[0]
    assert depth == 1, "the fused epilogue applies the final norm right after the single layer"
    out = _layer(x.reshape(b * s, d), b, s, norm_mix[0], w_in[0], cmp_pos_k[0], cmp_w1_k[0], cmp_w2_k[0],
                 cmp_pos_v[0], cmp_w1_v[0], cmp_w2_v[0], w_branch_nsa[0], w_branch_dsa[0], w_out[0],
                 norm_ffn[0], w_gate[0], w_up[0], w_down[0], norm_final)
    return out.reshape(b, s, d)
```

```python
import functools
import math

import numpy as np
import jax
import jax.numpy as jnp
from jax import lax
from jax.experimental import pallas as pl
from jax.experimental.pallas import tpu as pltpu

F32 = jnp.float32
BF16 = jnp.bfloat16

D_MODEL = 1024
HEAD_DIM = 64
ROPE_THETA = 10000.0
NORM_EPS = 1e-6
NEG_INF = -1e30
BIG = 1e4
_FMAX = 3.0e38
NSA_HEADS = 8
NSA_KV_HEADS = 2
NSA_GROUP = NSA_HEADS // NSA_KV_HEADS
CMP_BLOCK = 32
CMP_STRIDE = 16
CMP_HIDDEN = 256
SLC_BLOCK = 64
SLC_TOPN = 8
SLC_LOCAL = 2
WINDOW = 512
DSA_HEADS = 8
IDX_HEADS = 4
IDX_DIM = 64
DSA_TOPK_MAX = 256
D_FF = -(-8 * D_MODEL // (3 * 256)) * 256

LANES = 128
SUBLANES = 8
VMEM_LIMIT = 56 * 1024 * 1024

_O_QA, _O_KC, _O_VC, _O_KSL, _O_VSL, _O_KWN, _O_VWN = 0, 512, 640, 768, 896, 1024, 1152
_O_GA, _O_QB, _O_KB, _O_VB, _O_IQ, _O_IK, _O_IW, _O_GM = 1280, 1304, 1816, 1880, 1944, 2200, 2264, 2268
_D_IN = 4316
_N_QA, _N_KVC, _N_KVS, _N_KVW, _N_QB, _N_KVB, _N_IQ, _N_IK, _N_MISC, _N_GM = (
    0, 512, 768, 1024, 1280, 1792, 1920, 2176, 2304, 2432)
_D_IN_PAD = 4480
_MISC_GA, _MISC_IW = 0, 24


def _dot(a, b):
    return jnp.dot(a, b, preferred_element_type=F32)


def _dot_nt(a, b):
    return lax.dot_general(a, b, (((1,), (1,)), ((), ())), preferred_element_type=F32)


def _sigmoid(x):
    return 1.0 / (1.0 + jnp.exp(-x))


def _stack_heads(x, n_heads):
    tq = x.shape[0]
    low = lax.broadcasted_iota(jnp.int32, (tq, LANES), 1) < HEAD_DIM
    parts = []
    for p in range(n_heads // 2):
        slab = x[:, LANES * p:LANES * (p + 1)]
        parts.append(jnp.where(low, slab, 0.0))
        parts.append(jnp.where(low, pltpu.roll(slab, HEAD_DIM, 1), 0.0))
    return jnp.concatenate(parts, axis=0).astype(BF16)


def _unstack_heads(acc, n_heads):
    tq = acc.shape[0] // n_heads
    low = lax.broadcasted_iota(jnp.int32, (tq, LANES), 1) < HEAD_DIM
    outs = []
    for p in range(n_heads // 2):
        a0 = acc[(2 * p) * tq:(2 * p + 1) * tq]
        a1 = acc[(2 * p + 1) * tq:(2 * p + 2) * tq]
        outs.append(jnp.where(low, pltpu.roll(a0, HEAD_DIM, 1), a1))
    return jnp.concatenate(outs, axis=1)


def _softmax_once(q_all, n_heads, kvs, ovs, biases):
    per_group = n_heads // len(kvs)
    tq = q_all.shape[0] // n_heads
    out = []
    for j, (kv, ov, bias) in enumerate(zip(kvs, ovs, biases)):
        s_all = _dot_nt(q_all[j * per_group * tq:(j + 1) * per_group * tq], kv)
        p = []
        for r in range(per_group):
            s = s_all[r * tq:(r + 1) * tq] + bias
            p.append(jnp.exp2(s - jnp.max(s, axis=1, keepdims=True)).astype(BF16))
        pv = _dot(jnp.concatenate(p, axis=0), ov)
        out.extend(pv[r * tq:(r + 1) * tq] for r in range(per_group))
    return out


def _flash_loop(n_chunks, n_groups, q_ref, kv_at, bias_at, carry, scratch):
    s_ref, mx_ref, p_ref, m_ref, acc_ref = scratch
    n_heads, tq, _ = m_ref.shape
    tk = s_ref.shape[1]
    per_group = n_heads // n_groups
    grp_rows = [slice(j * per_group * tq, (j + 1) * per_group * tq) for j in range(n_groups)]
    head_rows = [slice(h * tq, (h + 1) * tq) for h in range(n_heads)]

    def scores(c, carry):
        kvs, _ = kv_at(c)
        biases, carry = bias_at(c, carry)
        out = []
        for j in range(n_groups):
            s_all = _dot_nt(q_ref[grp_rows[j]], kvs[j])
            for r in range(per_group):
                s = s_all[r * tq:(r + 1) * tq] + biases[j]
                out.append((s, jnp.broadcast_to(jnp.max(s, axis=1, keepdims=True), (tq, LANES))))
        return out, carry

    def put_scores(s):
        for h in range(n_heads):
            s_ref[head_rows[h]] = s[h][0]
            mx_ref[h] = s[h][1]

    def values(c, p, acc):
        _, ovs = kv_at(c)
        out = []
        for j in range(n_groups):
            pv = _dot(p[j], ovs[j])
            out.extend(acc[j * per_group + r] + pv[r * tq:(r + 1) * tq] for r in range(per_group))
        return out

    m_ref[...] = jnp.full(m_ref.shape, NEG_INF, F32)
    acc_ref[...] = jnp.zeros(acc_ref.shape, F32)
    p_ref[...] = jnp.zeros(p_ref.shape, BF16)
    s0, carry = scores(0, carry)
    put_scores(s0)

    def body(c, carry):
        acc_new = values(jnp.maximum(c - 1, 0), [p_ref[grp_rows[j]] for j in range(n_groups)],
                         [acc_ref[h] for h in range(n_heads)])
        m_new, p_new = [], []
        for h in range(n_heads):
            m_old, mx = m_ref[h], mx_ref[h]
            m_h = jnp.maximum(m_old, mx)
            acc_new[h] = acc_new[h] * jnp.exp2(jnp.minimum(m_old - mx, 0.0))
            m_wide = m_h if tk == LANES else jnp.concatenate([m_h] * (tk // LANES), axis=1)
            p_new.append(jnp.exp2(s_ref[head_rows[h]] - m_wide).astype(BF16))
            m_new.append(m_h)
        s_next, carry = scores(jnp.minimum(c + 1, n_chunks - 1), carry)
        for h in range(n_heads):
            acc_ref[h] = acc_new[h]
            m_ref[h] = m_new[h]
            p_ref[head_rows[h]] = p_new[h]
        put_scores(s_next)
        return carry

    lax.fori_loop(0, n_chunks, body, carry)
    return values(n_chunks - 1, [p_ref[grp_rows[j]] for j in range(n_groups)],
                  [acc_ref[h] for h in range(n_heads)])


def _flash_scratch(n_heads, tq, tk):
    return [pltpu.VMEM((n_heads * tq, tk), F32), pltpu.VMEM((n_heads, tq, LANES), F32),
            pltpu.VMEM((n_heads * tq, tk), BF16), pltpu.VMEM((n_heads, tq, LANES), F32),
            pltpu.VMEM((n_heads, tq, LANES), F32)]


def _flash_finish(accs):
    tq = accs[0].shape[0]
    low = lax.broadcasted_iota(jnp.int32, (tq, LANES), 1) < HEAD_DIM
    outs = []
    for p in range(len(accs) // 2):
        a0, a1 = accs[2 * p], accs[2 * p + 1]
        num = jnp.where(low, pltpu.roll(a0, HEAD_DIM, 1), a1)
        den = jnp.where(low, a0, pltpu.roll(a1, HEAD_DIM, 1))
        outs.append(num / den)
    return jnp.concatenate(outs, axis=1)


_ROPE_Q, _ROPE_IQ, _ROPE_K, _ROPE_KV = 0, 1, 2, 3


def _proj_kernel(x_ref, g_ref, w_ref, tab_ref, qa_ref, kvc_ref, kvs_ref, ovs_ref, kvw_ref, ovw_ref, qb_ref,
                 kvb_ref, ovb_ref, iq_ref, ik_ref, misc_ref, gm_ref):
    x = x_ref[...]
    tm = x.shape[0]
    ms = jnp.mean(x * x, axis=-1, keepdims=True)
    xn = (x * lax.rsqrt(ms + NORM_EPS) * g_ref[...]).astype(BF16)
    lane = lax.broadcasted_iota(jnp.int32, (tm, LANES), 1)
    low32 = (lane & (HEAD_DIM // 2)) == 0
    low64 = lane < HEAD_DIM

    def rope(z, kind):
        cos = tab_ref[2 * kind]
        sin = tab_ref[2 * kind + 1]
        rot = jnp.where(low32, pltpu.roll(z, LANES - HEAD_DIM // 2, 1), pltpu.roll(z, HEAD_DIM // 2, 1))
        return z * cos + rot * sin

    def emit(col, width, out_ref, out_col, kinds, ones_ref=None):
        z = _dot(xn, w_ref[:, col:col + width])
        for j in range(width // LANES):
            zj = z[:, LANES * j:LANES * (j + 1)]
            if kinds[j] is not None:
                zj = rope(zj, kinds[j])
            cols = slice(out_col + LANES * j, out_col + LANES * (j + 1))
            out_ref[:, cols] = zj.astype(out_ref.dtype)
            if ones_ref is not None:
                ones_ref[:, cols] = jnp.where(low64, 1.0, zj).astype(ones_ref.dtype)

    emit(_N_QA, 512, qa_ref, 0, [_ROPE_Q] * 4)
    emit(_N_KVC, 256, kvc_ref, 0, [_ROPE_K, None])
    emit(_N_KVS, 256, kvs_ref, 0, [_ROPE_KV] * 2, ovs_ref)
    emit(_N_KVW, 256, kvw_ref, 0, [_ROPE_KV] * 2, ovw_ref)
    emit(_N_QB, 512, qb_ref, 0, [_ROPE_Q] * 4)
    emit(_N_KVB, 128, kvb_ref, 0, [_ROPE_KV], ovb_ref)
    emit(_N_IQ, 256, iq_ref, 0, [_ROPE_IQ] * 2)
    emit(_N_IK, 128, ik_ref, 0, [_ROPE_K])
    emit(_N_MISC, 128, misc_ref, 0, [None])
    for c in range(4):
        emit(_N_GM + 512 * c, 512, gm_ref, 512 * c, [None] * 4)


def _relayout_w_in(w):
    z = lambda n: jnp.zeros((w.shape[0], n), w.dtype)
    s = lambda a, n: w[:, a:a + n]
    cols = [
        s(_O_QA, 512), s(_O_KC, 128), s(_O_VC, 128),
        s(_O_KSL, 64), s(_O_VSL, 64), s(_O_KSL + 64, 64), s(_O_VSL + 64, 64),
        s(_O_KWN, 64), s(_O_VWN, 64), s(_O_KWN + 64, 64), s(_O_VWN + 64, 64),
        s(_O_QB, 512), s(_O_KB, 64), s(_O_VB, 64), s(_O_IQ, 256),
        s(_O_IK, 64), z(64),
        s(_O_GA, 24), s(_O_IW, 4), z(100),
        s(_O_GM, 2048),
    ]
    out = jnp.concatenate(cols, axis=1)
    assert out.shape[1] == _D_IN_PAD
    return out.astype(BF16)


def _rope_tables(seq):
    half = HEAD_DIM // 2
    inv_freq = ROPE_THETA ** (-jnp.arange(half, dtype=F32) / half)
    ang = jnp.arange(seq, dtype=F32)[:, None] * inv_freq[None, :]
    cos, sin = jnp.cos(ang), jnp.sin(ang)
    cos64 = jnp.concatenate([cos, cos], axis=1)
    sin64 = jnp.concatenate([-sin, sin], axis=1)
    cos_k = jnp.concatenate([cos64, cos64], axis=1)
    sin_k = jnp.concatenate([sin64, sin64], axis=1)
    scale = HEAD_DIM ** -0.5
    q_scale = scale * math.log2(math.e)
    cos_kv = jnp.concatenate([cos64, jnp.ones_like(cos64)], axis=1)
    sin_kv = jnp.concatenate([sin64, jnp.zeros_like(sin64)], axis=1)
    return jnp.stack([cos_k * q_scale, sin_k * q_scale, cos_k * scale, sin_k * scale,
                      cos_k, sin_k, cos_kv, sin_kv], axis=0)


def _proj(x2, gain, w_pad, tabs, seq, tm):
    n = x2.shape[0]
    nblk_seq = seq // tm
    row = lambda width: pl.BlockSpec((tm, width), lambda i: (i, 0))
    full = lambda shape: pl.BlockSpec(shape, lambda i: (0,) * len(shape))
    widths = [(512, BF16), (256, F32), (256, BF16), (256, BF16), (256, BF16), (256, BF16), (512, BF16),
              (128, BF16), (128, BF16), (256, BF16), (128, BF16), (128, F32), (2048, F32)]
    return pl.pallas_call(
        _proj_kernel,
        grid=(n // tm,),
        in_specs=[row(D_MODEL), full((1, D_MODEL)), full((D_MODEL, _D_IN_PAD)),
                  pl.BlockSpec((tabs.shape[0], tm, LANES), lambda i: (0, i % nblk_seq, 0))],
        out_specs=[row(wd) for wd, _ in widths],
        out_shape=[jax.ShapeDtypeStruct((n, wd), dt) for wd, dt in widths],
        compiler_params=pltpu.CompilerParams(dimension_semantics=("arbitrary",), vmem_limit_bytes=VMEM_LIMIT),
    )(x2, gain, w_pad, tabs)


def _compress_kernel(c_ref, pos_ref, w1_ref, w2_ref, out_ref):
    n_chunk = c_ref.shape[3]
    for h in range(NSA_KV_HEADS):
        acc = jnp.zeros((n_chunk, LANES), F32)
        for kv in range(2):
            c = c_ref[0, kv, h]
            a_lo = (c + pos_ref[kv, 0]).astype(BF16)
            a_hi = (c + pos_ref[kv, 1]).astype(BF16)
            half = CMP_STRIDE * HEAD_DIM
            h_lo = _dot(a_lo, w1_ref[kv, :half, :])
            h_hi = _dot(a_hi, w1_ref[kv, half:, :])
            hid = h_lo + pltpu.roll(h_hi, n_chunk - 1, 0)
            act = jax.nn.gelu(hid, approximate=True).astype(BF16)
            acc = acc + _dot(act, w2_ref[kv])
        out_ref[0, h] = acc.astype(out_ref.dtype)


def _compress(c, pos, w1, w2):
    b = c.shape[0]
    n_chunk = c.shape[3]
    return pl.pallas_call(
        _compress_kernel,
        grid=(b,),
        in_specs=[pl.BlockSpec((1,) + c.shape[1:], lambda i: (i, 0, 0, 0, 0)),
                  pl.BlockSpec(pos.shape, lambda i: (0, 0, 0, 0)),
                  pl.BlockSpec(w1.shape, lambda i: (0, 0, 0)),
                  pl.BlockSpec(w2.shape, lambda i: (0, 0, 0))],
        out_specs=pl.BlockSpec((1, NSA_KV_HEADS, n_chunk, LANES), lambda i: (i, 0, 0, 0)),
        out_shape=jax.ShapeDtypeStruct((b, NSA_KV_HEADS, n_chunk, LANES), BF16),
        compiler_params=pltpu.CompilerParams(dimension_semantics=("arbitrary",), vmem_limit_bytes=VMEM_LIMIT),
    )(c, pos, w1, w2)


def _cmp_kernel(qa_ref, kvc_ref, ot_ref, ocmp_ref, selt_ref):
    tq = qa_ref.shape[1]
    n_cmp = kvc_ref.shape[2]
    n_slc = ot_ref.shape[0]
    q0 = pl.program_id(1) * tq
    q = qa_ref[0].astype(F32)
    t3 = q0 + lax.broadcasted_iota(jnp.int32, (NSA_GROUP, tq, n_cmp), 1)
    n3 = lax.broadcasted_iota(jnp.int32, (NSA_GROUP, tq, n_cmp), 2)
    mask = (n3 * CMP_STRIDE + (CMP_BLOCK - 1)) <= t3
    j = lax.broadcasted_iota(jnp.int32, (n_slc, tq), 0)
    cur = (q0 + lax.broadcasted_iota(jnp.int32, (n_slc, tq), 1)) // SLC_BLOCK
    forced = (j == 0) | ((cur - j >= 0) & (cur - j < SLC_LOCAL))
    adm = j <= cur
    outs = []
    for g in range(NSA_KV_HEADS):
        gw = NSA_GROUP * HEAD_DIM
        q4 = _stack_heads(q[:, gw * g:gw * (g + 1)], NSA_GROUP)
        kv = kvc_ref[0, g]
        s = _dot_nt(q4, kv).reshape(NSA_GROUP, tq, n_cmp)
        s = jnp.where(mask, s, NEG_INF)
        m = jnp.max(s, axis=2, keepdims=True)
        e = jnp.where(mask, jnp.exp2(s - m), 0.0)
        l = jnp.sum(e, axis=2, keepdims=True)
        p = e * jnp.where(l > 0.0, 1.0 / l, 0.0)
        o = _dot(p.reshape(NSA_GROUP * tq, n_cmp).astype(BF16), kv)
        outs.append(_unstack_heads(o, NSA_GROUP))
        psum = p[0] + p[1] + p[2] + p[3]
        hi = psum.astype(BF16)
        lo = (psum - hi.astype(F32)).astype(BF16)
        p_slc = _dot_nt(ot_ref[...], hi) + _dot_nt(ot_ref[...], lo)
        blk = jnp.where(forced, BIG, p_slc)
        blk = jnp.where(adm, blk, NEG_INF)
        rank = jnp.zeros((n_slc, tq), F32)
        for i in range(n_slc):
            bi = blk[i:i + 1, :]
            beats = (bi > blk) | ((bi == blk) & (j > i))
            rank = rank + jnp.where(beats, 1.0, 0.0)
        selt_ref[0, g] = jnp.where((rank < float(SLC_TOPN)) & adm, 1.0, 0.0)
    ocmp_ref[0] = jnp.concatenate(outs, axis=1)


def _cmp(qa, kvc, ot, tq):
    b, s, _ = qa.shape
    n_cmp = kvc.shape[2]
    n_slc = ot.shape[0]
    return pl.pallas_call(
        _cmp_kernel,
        grid=(b, s // tq),
        in_specs=[pl.BlockSpec((1, tq, NSA_HEADS * HEAD_DIM), lambda i, j: (i, j, 0)),
                  pl.BlockSpec((1, NSA_KV_HEADS, n_cmp, LANES), lambda i, j: (i, 0, 0, 0)),
                  pl.BlockSpec(ot.shape, lambda i, j: (0, 0))],
        out_specs=[pl.BlockSpec((1, tq, NSA_HEADS * HEAD_DIM), lambda i, j: (i, j, 0)),
                   pl.BlockSpec((1, NSA_KV_HEADS, n_slc, tq), lambda i, j: (i, 0, 0, j))],
        out_shape=[jax.ShapeDtypeStruct((b, s, NSA_HEADS * HEAD_DIM), F32),
                   jax.ShapeDtypeStruct((b, NSA_KV_HEADS, n_slc, s), F32)],
        compiler_params=pltpu.CompilerParams(dimension_semantics=("arbitrary", "arbitrary"),
                                             vmem_limit_bytes=VMEM_LIMIT),
    )(qa, kvc, ot)


def _nsa_kernel(qa_ref, kvs_ref, ovs_ref, kvw_ref, ovw_ref, sel_ref, e_ref, ocmp_ref, misc_ref, gx_ref, o_ref,
                q_ref, *flash):
    tk = flash[0].shape[1]
    tq = qa_ref.shape[1]
    qi = pl.program_id(1)
    q0 = qi * tq
    span = WINDOW + tq
    gsig = _sigmoid(misc_ref[0])
    ocmp = ocmp_ref[0]
    t_s = q0 + lax.broadcasted_iota(jnp.int32, (tq, tk), 0)
    lane_s = lax.broadcasted_iota(jnp.int32, (tq, tk), 1)
    hw = NSA_HEADS * HEAD_DIM
    groups = range(NSA_KV_HEADS)
    cols = [slice(LANES * g, LANES * (g + 1)) for g in groups]
    g_hi = gsig.astype(BF16)
    g_lo = (gsig - g_hi.astype(F32)).astype(BF16)
    gates = _dot(g_hi, gx_ref[...]) + _dot(g_lo, gx_ref[...])

    def gate(branch):
        return gates[:, hw * branch:hw * (branch + 1)]

    q_ref[...] = _stack_heads(qa_ref[0].astype(F32), NSA_HEADS)

    def slc_kv(kt):
        k0 = pl.multiple_of(kt * tk, tk)
        return ([kvs_ref[0, pl.ds(k0, tk), cols[g]] for g in groups],
                [ovs_ref[0, pl.ds(k0, tk), cols[g]] for g in groups])

    def slc_bias(kt, carry):
        causal = (kt * tk + lane_s) <= t_s
        return [jnp.where((_dot(sel_ref[0, g], e_ref[kt]) > 0.5) & causal, 0.0, NEG_INF) for g in groups], carry

    o_slc = _flash_finish(_flash_loop((q0 + tq + tk - 1) // tk, NSA_KV_HEADS, q_ref, slc_kv, slc_bias, 0, flash))

    k_lo = pl.multiple_of(jnp.maximum(q0 + tq - span, 0), tq)
    dist = (q0 + lax.broadcasted_iota(jnp.int32, (tq, span), 0)) - (k_lo + lax.broadcasted_iota(
        jnp.int32, (tq, span), 1))
    bias_w = jnp.where((dist >= 0) & (dist < WINDOW), 0.0, NEG_INF)
    o_win = _flash_finish(_softmax_once(
        q_ref[...], NSA_HEADS, [kvw_ref[0, pl.ds(k_lo, span), cols[g]] for g in groups],
        [ovw_ref[0, pl.ds(k_lo, span), cols[g]] for g in groups], [bias_w] * NSA_KV_HEADS))

    o_ref[0] = (gate(0) * ocmp + gate(1) * o_slc + gate(2) * o_win).astype(o_ref.dtype)


def _nsa(qa, kvs, ovs, kvw, ovw, sel, e, ocmp, misc, tq, tk):
    b, s, _ = qa.shape
    n_slc = sel.shape[3]
    hw = NSA_HEADS * HEAD_DIM
    assert WINDOW % tq == 0 and WINDOW + tq <= s
    gx = np.zeros((LANES, 3 * hw), np.float32)
    for branch in range(3):
        for h in range(NSA_HEADS):
            c0 = branch * hw + HEAD_DIM * h
            gx[_MISC_GA + branch * NSA_HEADS + h, c0:c0 + HEAD_DIM] = 1.0
    gx = jnp.asarray(gx, BF16)
    seq = lambda: pl.BlockSpec((1, s, 2 * LANES), lambda i, j: (i, 0, 0))
    return pl.pallas_call(
        _nsa_kernel,
        grid=(b, s // tq),
        in_specs=[pl.BlockSpec((1, tq, hw), lambda i, j: (i, j, 0)),
                  seq(), seq(), seq(), seq(),
                  pl.BlockSpec((1, NSA_KV_HEADS, tq, n_slc), lambda i, j: (i, 0, j, 0)),
                  pl.BlockSpec(e.shape, lambda i, j: (0, 0, 0)),
                  pl.BlockSpec((1, tq, hw), lambda i, j: (i, j, 0)),
                  pl.BlockSpec((1, tq, LANES), lambda i, j: (i, j, 0)),
                  pl.BlockSpec(gx.shape, lambda i, j: (0, 0))],
        out_specs=pl.BlockSpec((1, tq, hw), lambda i, j: (i, j, 0)),
        out_shape=jax.ShapeDtypeStruct((b, s, hw), BF16),
        scratch_shapes=[pltpu.VMEM((NSA_HEADS * tq, LANES), BF16)] + _flash_scratch(NSA_HEADS, tq, tk),
        compiler_params=pltpu.CompilerParams(dimension_semantics=("arbitrary", "arbitrary"),
                                             vmem_limit_bytes=VMEM_LIMIT),
    )(qa, kvs, ovs, kvw, ovw, sel, e, ocmp, misc, gx)


def _dsa_kernel(qb_ref, kvb_ref, ovb_ref, iq_ref, ik_ref, misc_ref, l2_ref, o_ref,
                score_ref, q_ref, *flash, topk, n_bisect):
    tq = qb_ref.shape[1]
    ts = score_ref.shape[1]
    ta = 2 * ts
    assert ts == tq == LANES
    fold = ts // SUBLANES
    qi = pl.program_id(1)
    q0 = qi * tq
    n_pair = (qi + 2) // 2
    key = lax.broadcasted_iota(jnp.int32, (ts, tq), 0)
    qry = lax.broadcasted_iota(jnp.int32, (ts, tq), 1)

    def fold_min(x):
        return jnp.min(x.reshape(fold, SUBLANES, tq), axis=0)

    def fold_max(x):
        return jnp.max(x.reshape(fold, SUBLANES, tq), axis=0)

    def fold_sum(x):
        return jnp.sum(x.reshape(fold, SUBLANES, tq), axis=0)

    q_ref[0:IDX_HEADS * tq] = _stack_heads(iq_ref[0].astype(F32), IDX_HEADS)
    misc_t = misc_ref[0].T
    wts = [misc_t[_MISC_IW + h:_MISC_IW + h + 1, :] * (IDX_HEADS ** -0.5) for h in range(IDX_HEADS)]

    def logits(c):
        k0 = pl.multiple_of(c * ta, ta)
        return _dot_nt(ik_ref[0, pl.ds(k0, ta), :], q_ref[0:IDX_HEADS * tq])

    def score_body(c, carry):
        mn, mx = carry
        k0 = c * ta
        lg = logits(c)
        for j in range(2):
            rows = slice(j * ts, (j + 1) * ts)
            sc = wts[0] * jnp.maximum(lg[rows, 0:tq], 0.0)
            for h in range(1, IDX_HEADS):
                sc = sc + wts[h] * jnp.maximum(lg[rows, h * tq:(h + 1) * tq], 0.0)
            causal = (k0 + j * ts + key) <= (q0 + qry)
            score_ref[2 * c + j] = jnp.where(causal, sc, NEG_INF)
            mn = jnp.minimum(mn, fold_min(jnp.where(causal, sc, _FMAX)))
            mx = jnp.maximum(mx, fold_max(jnp.where(causal, sc, -_FMAX)))
        return mn, mx

    mn, mx = lax.fori_loop(0, n_pair, score_body, (jnp.full((SUBLANES, tq), _FMAX, F32),
                                                   jnp.full((SUBLANES, tq), -_FMAX, F32)))

    @pl.when(n_pair % 2 == 1)
    def _():
        score_ref[2 * n_pair] = jnp.full((ts, tq), NEG_INF, F32)
        score_ref[2 * n_pair + 1] = jnp.full((ts, tq), NEG_INF, F32)
    lo = jnp.min(mn, axis=0, keepdims=True)
    hi = jnp.max(mx, axis=0, keepdims=True)
    kf = jnp.minimum(q0 + lax.broadcasted_iota(jnp.int32, (1, tq), 1) + 1, topk).astype(F32)

    def midpoint(lo, hi):
        mid = lo + (hi - lo) * 0.5
        return jnp.where(mid >= hi, lo, mid)

    def count_gt(thr):
        def body(c, cnt):
            for j in range(4):
                cnt = cnt + fold_sum(jnp.where(score_ref[4 * c + j] > thr, 1.0, 0.0))
            return cnt
        cnt = lax.fori_loop(0, (n_pair + 1) // 2, body, jnp.zeros((SUBLANES, tq), F32))
        return jnp.sum(cnt, axis=0, keepdims=True)

    def bisect(_, carry):
        lo, hi = carry
        mid = midpoint(lo, hi)
        up = count_gt(mid) >= kf
        return jnp.where(up, mid, lo), jnp.where(up, hi, mid)

    lo, hi = lax.fori_loop(0, n_bisect, bisect, (lo, hi))

    def snap_cond(carry):
        lo, hi = carry
        return jnp.max(hi - lo) > 0.0

    def snap(carry):
        lo, hi = carry
        mid = midpoint(lo, hi)

        def body(c, carry):
            cnt, above, below = carry
            for j in range(2):
                sc = score_ref[2 * c + j]
                gt = sc > mid
                cnt = cnt + fold_sum(jnp.where(gt, 1.0, 0.0))
                above = jnp.minimum(above, fold_min(jnp.where(gt, sc, _FMAX)))
                below = jnp.maximum(below, fold_max(jnp.where(gt, -_FMAX, sc)))
            return cnt, above, below

        cnt, above, below = lax.fori_loop(
            0, n_pair, body, (jnp.zeros((SUBLANES, tq), F32), jnp.full((SUBLANES, tq), _FMAX, F32),
                              jnp.full((SUBLANES, tq), -_FMAX, F32)))
        up = jnp.sum(cnt, axis=0, keepdims=True) >= kf
        return (jnp.where(up, jnp.min(above, axis=0, keepdims=True), lo),
                jnp.where(up, hi, jnp.max(below, axis=0, keepdims=True)))

    v, _ = lax.while_loop(snap_cond, lambda carry: snap(snap(carry)), (lo, hi))
    need = kf - count_gt(v)

    q_ref[...] = _stack_heads(qb_ref[0].astype(F32), DSA_HEADS)

    def att_kv(c):
        k0 = pl.multiple_of(c * ta, ta)
        return [kvb_ref[0, pl.ds(k0, ta), :]], [ovb_ref[0, pl.ds(k0, ta), :]]

    def att_bias(c, run):
        halves = []
        for j in range(2):
            sc = score_ref[2 * c + j]
            eq = sc == v
            pt = _dot(l2_ref[...], jnp.where(eq, 1.0, 0.0).astype(BF16))
            take = (sc > v) | (eq & (run + pt[:ts] < need))
            halves.append(jnp.where(take, 0.0, NEG_INF).T)
            run = run + pt[ts:ts + 1]
        return [jnp.concatenate(halves, axis=1)], run

    accs = _flash_loop(n_pair, 1, q_ref, att_kv, att_bias, jnp.zeros((1, tq), F32), flash)
    o_ref[0] = _flash_finish(accs).astype(o_ref.dtype)


def _dsa(qb, kvb, ovb, iq, ik, misc, l2, tq, topk, n_bisect):
    b, s, _ = qb.shape
    hw = DSA_HEADS * HEAD_DIM
    assert (s // tq) % 4 == 0
    return pl.pallas_call(
        functools.partial(_dsa_kernel, topk=topk, n_bisect=n_bisect),
        grid=(b, s // tq),
        in_specs=[pl.BlockSpec((1, tq, hw), lambda i, j: (i, j, 0)),
                  pl.BlockSpec((1, s, LANES), lambda i, j: (i, 0, 0)),
                  pl.BlockSpec((1, s, LANES), lambda i, j: (i, 0, 0)),
                  pl.BlockSpec((1, tq, IDX_HEADS * IDX_DIM), lambda i, j: (i, j, 0)),
                  pl.BlockSpec((1, s, LANES), lambda i, j: (i, 0, 0)),
                  pl.BlockSpec((1, tq, LANES), lambda i, j: (i, j, 0)),
                  pl.BlockSpec(l2.shape, lambda i, j: (0, 0))],
        out_specs=pl.BlockSpec((1, tq, hw), lambda i, j: (i, j, 0)),
        out_shape=jax.ShapeDtypeStruct((b, s, hw), BF16),
        scratch_shapes=[pltpu.VMEM((s // tq, tq, tq), F32),
                        pltpu.VMEM((DSA_HEADS * tq, LANES), BF16)] + _flash_scratch(DSA_HEADS, tq, 2 * tq),
        compiler_params=pltpu.CompilerParams(dimension_semantics=("arbitrary", "arbitrary"),
                                             vmem_limit_bytes=VMEM_LIMIT),
    )(qb, kvb, ovb, iq, ik, misc, l2)


def _post_kernel(on_ref, od_ref, gm_ref, x_ref, wbn_ref, wbd_ref, wo_ref, gffn_ref, wg_ref, wu_ref, wd_ref,
                 gfin_ref, out_ref):
    ya = _dot(on_ref[...], wbn_ref[...])
    yb = _dot(od_ref[...], wbd_ref[...])
    gm = gm_ref[...]
    merged = _sigmoid(gm[:, :D_MODEL]) * ya + _sigmoid(gm[:, D_MODEL:]) * yb
    h = x_ref[...] + _dot(merged.astype(BF16), wo_ref[...])
    ms = jnp.mean(h * h, axis=-1, keepdims=True)
    hn = (h * lax.rsqrt(ms + NORM_EPS) * gffn_ref[...]).astype(BF16)
    acc = jnp.zeros_like(h)
    for c in range(wg_ref.shape[0]):
        gt = _dot(hn, wg_ref[c])
        up = _dot(hn, wu_ref[c])
        act = (gt * _sigmoid(gt) * up).astype(BF16)
        acc = acc + _dot(act, wd_ref[c])
    h2 = h + acc
    ms2 = jnp.mean(h2 * h2, axis=-1, keepdims=True)
    out_ref[...] = h2 * lax.rsqrt(ms2 + NORM_EPS) * gfin_ref[...]


def _post(o_nsa, o_dsa, gm, x2, wbn, wbd, wo, gffn, wg, wu, wd, gfin, tm):
    n = x2.shape[0]
    row = lambda width: pl.BlockSpec((tm, width), lambda i: (i, 0))
    full = lambda a: pl.BlockSpec(a.shape, lambda i: (0,) * a.ndim, pipeline_mode=pl.Buffered(1))
    return pl.pallas_call(
        _post_kernel,
        grid=(n // tm,),
        in_specs=[row(o_nsa.shape[1]), row(o_dsa.shape[1]), row(2 * D_MODEL), row(D_MODEL),
                  full(wbn), full(wbd), full(wo), full(gffn), full(wg), full(wu), full(wd), full(gfin)],
        out_specs=row(D_MODEL),
        out_shape=jax.ShapeDtypeStruct((n, D_MODEL), F32),
        compiler_params=pltpu.CompilerParams(dimension_semantics=("arbitrary",), vmem_limit_bytes=VMEM_LIMIT),
    )(o_nsa, o_dsa, gm, x2, wbn, wbd, wo, gffn, wg, wu, wd, gfin)


def _layer(h2, b, s, norm_mix, w_in, cmp_pos_k, cmp_w1_k, cmp_w2_k, cmp_pos_v, cmp_w1_v, cmp_w2_v,
           w_branch_nsa, w_branch_dsa, w_out, norm_ffn, w_gate, w_up, w_down, norm_out):
    tq = LANES
    n_chunk = s // CMP_STRIDE
    n_slc = s // SLC_BLOCK
    tabs = _rope_tables(s)
    qa, kvc, kvs, ovs, kvw, ovw, qb, kvb, ovb, iq, ik, misc, gm = _proj(
        h2, norm_mix[None, :], _relayout_w_in(w_in), tabs, s, 512)
    seq3 = lambda a: a.reshape(b, s, -1)

    c = kvc.reshape(b, n_chunk, CMP_STRIDE, 2, NSA_KV_HEADS, HEAD_DIM).transpose(0, 3, 4, 1, 2, 5)
    c = c.reshape(b, 2, NSA_KV_HEADS, n_chunk, CMP_STRIDE * HEAD_DIM)
    pos = jnp.stack([cmp_pos_k, cmp_pos_v]).reshape(2, 2, 1, CMP_STRIDE * HEAD_DIM)
    w1 = jnp.stack([cmp_w1_k, cmp_w1_v]).astype(BF16)
    zpad = jnp.zeros((CMP_HIDDEN, HEAD_DIM), F32)
    w2 = jnp.stack([jnp.concatenate([cmp_w2_k, zpad], axis=1),
                    jnp.concatenate([zpad, cmp_w2_v], axis=1)]).astype(BF16)
    kvcmp = _compress(c, pos, w1, w2)

    ci = np.arange(n_chunk)[None, :] * CMP_STRIDE
    sj = np.arange(n_slc)[:, None] * SLC_BLOCK
    ot = ((ci < sj + SLC_BLOCK) & (ci + CMP_BLOCK > sj) & (np.arange(n_chunk)[None, :] < n_chunk - 1))
    ocmp, selt = _cmp(seq3(qa), kvcmp, jnp.asarray(ot, BF16), tq)
    sel = selt.transpose(0, 1, 3, 2).astype(BF16)

    tk = 256
    e = (np.arange(s)[None, :] // SLC_BLOCK == np.arange(n_slc)[:, None])
    e = jnp.asarray(e.reshape(n_slc, s // tk, tk).transpose(1, 0, 2), BF16)
    o_nsa = _nsa(seq3(qa), seq3(kvs), seq3(ovs), seq3(kvw), seq3(ovw), sel, e, ocmp, seq3(misc), tq, tk)

    idx = np.arange(tq)
    l2 = jnp.asarray(np.concatenate([idx[:, None] > idx[None, :], np.ones((tq, tq), bool)], axis=0), BF16)
    o_dsa = _dsa(seq3(qb), seq3(kvb), seq3(ovb), seq3(iq), seq3(ik), seq3(misc), l2, tq,
                 min(DSA_TOPK_MAX, s // 4), 12)

    fc = 256
    wg = w_gate.astype(BF16).reshape(D_MODEL, D_FF // fc, fc).transpose(1, 0, 2)
    wu = w_up.astype(BF16).reshape(D_MODEL, D_FF // fc, fc).transpose(1, 0, 2)
    wd = w_down.astype(BF16).reshape(D_FF // fc, fc, D_MODEL)
    return _post(o_nsa.reshape(b * s, -1), o_dsa.reshape(b * s, -1), gm, h2,
                 w_branch_nsa.astype(BF16), w_branch_dsa.astype(BF16), w_out.astype(BF16),
                 norm_ffn[None, :], wg, wu, wd, norm_out[None, :], 512)


def kernel(x, norm_mix, w_in, cmp_pos_k, cmp_w1_k, cmp_w2_k, cmp_pos_v, cmp_w1_v, cmp_w2_v, w_branch_nsa,
           w_branch_dsa, w_out, norm_ffn, w_gate, w_up, w_down, norm_final):
    b, s, d = x.shape
    depth = norm_mix.shape[0]
    assert depth == 1, "the fused epilogue applies the final norm right after the single layer"
    out = _layer(x.reshape(b * s, d), b, s, norm_mix[0], w_in[0], cmp_pos_k[0], cmp_w1_k[0], cmp_w2_k[0],
                 cmp_pos_v[0], cmp_w1_v[0], cmp_w2_v[0], w_branch_nsa[0], w_branch_dsa[0], w_out[0],
                 norm_ffn[0], w_gate[0], w_up[0], w_down[0], norm_final)
    return out.reshape(b, s, d)
```

```python
import functools
import math

import numpy as np
import jax
import jax.numpy as jnp
from jax import lax
from jax.experimental import pallas as pl
from jax.experimental.pallas import tpu as pltpu

F32 = jnp.float32
BF16 = jnp.bfloat16

D_MODEL = 1024
HEAD_DIM = 64
ROPE_THETA = 10000.0
NORM_EPS = 1e-6
NEG_INF = -1e30
BIG = 1e4
_FMAX = 3.0e38
NSA_HEADS = 8
NSA_KV_HEADS = 2
NSA_GROUP = NSA_HEADS // NSA_KV_HEADS
CMP_BLOCK = 32
CMP_STRIDE = 16
CMP_HIDDEN = 256
SLC_BLOCK = 64
SLC_TOPN = 8
SLC_LOCAL = 2
WINDOW = 512
DSA_HEADS = 8
IDX_HEADS = 4
IDX_DIM = 64
DSA_TOPK_MAX = 256
D_FF = -(-8 * D_MODEL // (3 * 256)) * 256

LANES = 128
SUBLANES = 8
VMEM_LIMIT = 56 * 1024 * 1024

_O_QA, _O_KC, _O_VC, _O_KSL, _O_VSL, _O_KWN, _O_VWN = 0, 512, 640, 768, 896, 1024, 1152
_O_GA, _O_QB, _O_KB, _O_VB, _O_IQ, _O_IK, _O_IW, _O_GM = 1280, 1304, 1816, 1880, 1944, 2200, 2264, 2268
_D_IN = 4316
_N_QA, _N_KVC, _N_KVS, _N_KVW, _N_QB, _N_KVB, _N_IQ, _N_IK, _N_MISC, _N_GM = (
    0, 512, 768, 1024, 1280, 1792, 1920, 2176, 2304, 2432)
_D_IN_PAD = 4480
_MISC_GA, _MISC_IW = 0, 24


def _dot(a, b):
    return jnp.dot(a, b, preferred_element_type=F32)


def _dot_nt(a, b):
    return lax.dot_general(a, b, (((1,), (1,)), ((), ())), preferred_element_type=F32)


def _sigmoid(x):
    return 1.0 / (1.0 + jnp.exp(-x))


def _stack_heads(x, n_heads):
    tq = x.shape[0]
    low = lax.broadcasted_iota(jnp.int32, (tq, LANES), 1) < HEAD_DIM
    parts = []
    for p in range(n_heads // 2):
        slab = x[:, LANES * p:LANES * (p + 1)]
        parts.append(jnp.where(low, slab, 0.0))
        parts.append(jnp.where(low, pltpu.roll(slab, HEAD_DIM, 1), 0.0))
    return jnp.concatenate(parts, axis=0).astype(BF16)


def _unstack_heads(acc, n_heads):
    tq = acc.shape[0] // n_heads
    low = lax.broadcasted_iota(jnp.int32, (tq, LANES), 1) < HEAD_DIM
    outs = []
    for p in range(n_heads // 2):
        a0 = acc[(2 * p) * tq:(2 * p + 1) * tq]
        a1 = acc[(2 * p + 1) * tq:(2 * p + 2) * tq]
        outs.append(jnp.where(low, pltpu.roll(a0, HEAD_DIM, 1), a1))
    return jnp.concatenate(outs, axis=1)


def _softmax_once(q_all, n_heads, kvs, ovs, biases):
    per_group = n_heads // len(kvs)
    tq = q_all.shape[0] // n_heads
    out = []
    for j, (kv, ov, bias) in enumerate(zip(kvs, ovs, biases)):
        s_all = _dot_nt(q_all[j * per_group * tq:(j + 1) * per_group * tq], kv)
        p = []
        for r in range(per_group):
            s = s_all[r * tq:(r + 1) * tq] + bias
            p.append(jnp.exp2(s - jnp.max(s, axis=1, keepdims=True)).astype(BF16))
        pv = _dot(jnp.concatenate(p, axis=0), ov)
        out.extend(pv[r * tq:(r + 1) * tq] for r in range(per_group))
    return out


def _flash_loop(n_chunks, n_groups, q_ref, kv_at, bias_at, carry, scratch):
    s_ref, mx_ref, p_ref, m_ref, acc_ref = scratch
    n_heads, tq, _ = m_ref.shape
    tk = s_ref.shape[1]
    per_group = n_heads // n_groups
    grp_rows = [slice(j * per_group * tq, (j + 1) * per_group * tq) for j in range(n_groups)]
    head_rows = [slice(h * tq, (h + 1) * tq) for h in range(n_heads)]

    def scores(c, carry):
        kvs, _ = kv_at(c)
        biases, carry = bias_at(c, carry)
        out = []
        for j in range(n_groups):
            s_all = _dot_nt(q_ref[grp_rows[j]], kvs[j])
            for r in range(per_group):
                s = s_all[r * tq:(r + 1) * tq] + biases[j]
                out.append((s, jnp.broadcast_to(jnp.max(s, axis=1, keepdims=True), (tq, LANES))))
        return out, carry

    def put_scores(s):
        for h in range(n_heads):
            s_ref[head_rows[h]] = s[h][0]
            mx_ref[h] = s[h][1]

    def values(c, p, acc):
        _, ovs = kv_at(c)
        out = []
        for j in range(n_groups):
            pv = _dot(p[j], ovs[j])
            out.extend(acc[j * per_group + r] + pv[r * tq:(r + 1) * tq] for r in range(per_group))
        return out

    m_ref[...] = jnp.full(m_ref.shape, NEG_INF, F32)
    acc_ref[...] = jnp.zeros(acc_ref.shape, F32)
    p_ref[...] = jnp.zeros(p_ref.shape, BF16)
    s0, carry = scores(0, carry)
    put_scores(s0)

    def body(c, carry):
        acc_new = values(jnp.maximum(c - 1, 0), [p_ref[grp_rows[j]] for j in range(n_groups)],
                         [acc_ref[h] for h in range(n_heads)])
        m_new, p_new = [], []
        for h in range(n_heads):
            d = jnp.minimum(m_ref[h] - mx_ref[h], 0.0)
            m_h = m_ref[h] - d
            acc_new[h] = acc_new[h] * jnp.exp2(d)
            m_wide = m_h if tk == LANES else jnp.concatenate([m_h] * (tk // LANES), axis=1)
            p_new.append(jnp.exp2(s_ref[head_rows[h]] - m_wide).astype(BF16))
            m_new.append(m_h)
        s_next, carry = scores(jnp.minimum(c + 1, n_chunks - 1), carry)
        for h in range(n_heads):
            acc_ref[h] = acc_new[h]
            m_ref[h] = m_new[h]
            p_ref[head_rows[h]] = p_new[h]
        put_scores(s_next)
        return carry

    lax.fori_loop(0, n_chunks, body, carry)
    return values(n_chunks - 1, [p_ref[grp_rows[j]] for j in range(n_groups)],
                  [acc_ref[h] for h in range(n_heads)])


def _flash_scratch(n_heads, tq, tk):
    return [pltpu.VMEM((n_heads * tq, tk), F32), pltpu.VMEM((n_heads, tq, LANES), F32),
            pltpu.VMEM((n_heads * tq, tk), BF16), pltpu.VMEM((n_heads, tq, LANES), F32),
            pltpu.VMEM((n_heads, tq, LANES), F32)]


def _flash_finish(accs):
    tq = accs[0].shape[0]
    low = lax.broadcasted_iota(jnp.int32, (tq, LANES), 1) < HEAD_DIM
    outs = []
    for p in range(len(accs) // 2):
        a0, a1 = accs[2 * p], accs[2 * p + 1]
        num = jnp.where(low, pltpu.roll(a0, HEAD_DIM, 1), a1)
        den = jnp.where(low, a0, pltpu.roll(a1, HEAD_DIM, 1))
        outs.append(num / den)
    return jnp.concatenate(outs, axis=1)


_ROPE_Q, _ROPE_IQ, _ROPE_K, _ROPE_KV = 0, 1, 2, 3


def _proj_kernel(x_ref, g_ref, w_ref, tab_ref, qa_ref, kvc_ref, kvs_ref, ovs_ref, kvw_ref, ovw_ref, qb_ref,
                 kvb_ref, ovb_ref, iq_ref, ik_ref, misc_ref, gm_ref):
    x = x_ref[...]
    tm = x.shape[0]
    ms = jnp.mean(x * x, axis=-1, keepdims=True)
    xn = (x * lax.rsqrt(ms + NORM_EPS) * g_ref[...]).astype(BF16)
    lane = lax.broadcasted_iota(jnp.int32, (tm, LANES), 1)
    low32 = (lane & (HEAD_DIM // 2)) == 0
    low64 = lane < HEAD_DIM

    def rope(z, kind):
        cos = tab_ref[2 * kind]
        sin = tab_ref[2 * kind + 1]
        rot = jnp.where(low32, pltpu.roll(z, LANES - HEAD_DIM // 2, 1), pltpu.roll(z, HEAD_DIM // 2, 1))
        return z * cos + rot * sin

    def emit(col, width, out_ref, out_col, kinds, ones_ref=None):
        z = _dot(xn, w_ref[:, col:col + width])
        for j in range(width // LANES):
            zj = z[:, LANES * j:LANES * (j + 1)]
            if kinds[j] is not None:
                zj = rope(zj, kinds[j])
            cols = slice(out_col + LANES * j, out_col + LANES * (j + 1))
            out_ref[:, cols] = zj.astype(out_ref.dtype)
            if ones_ref is not None:
                ones_ref[:, cols] = jnp.where(low64, 1.0, zj).astype(ones_ref.dtype)

    emit(_N_QA, 512, qa_ref, 0, [_ROPE_Q] * 4)
    emit(_N_KVC, 256, kvc_ref, 0, [_ROPE_K, None])
    emit(_N_KVS, 256, kvs_ref, 0, [_ROPE_KV] * 2, ovs_ref)
    emit(_N_KVW, 256, kvw_ref, 0, [_ROPE_KV] * 2, ovw_ref)
    emit(_N_QB, 512, qb_ref, 0, [_ROPE_Q] * 4)
    emit(_N_KVB, 128, kvb_ref, 0, [_ROPE_KV], ovb_ref)
    emit(_N_IQ, 256, iq_ref, 0, [_ROPE_IQ] * 2)
    emit(_N_IK, 128, ik_ref, 0, [_ROPE_K])
    emit(_N_MISC, 128, misc_ref, 0, [None])
    for c in range(4):
        emit(_N_GM + 512 * c, 512, gm_ref, 512 * c, [None] * 4)


def _relayout_w_in(w):
    z = lambda n: jnp.zeros((w.shape[0], n), w.dtype)
    s = lambda a, n: w[:, a:a + n]
    cols = [
        s(_O_QA, 512), s(_O_KC, 128), s(_O_VC, 128),
        s(_O_KSL, 64), s(_O_VSL, 64), s(_O_KSL + 64, 64), s(_O_VSL + 64, 64),
        s(_O_KWN, 64), s(_O_VWN, 64), s(_O_KWN + 64, 64), s(_O_VWN + 64, 64),
        s(_O_QB, 512), s(_O_KB, 64), s(_O_VB, 64), s(_O_IQ, 256),
        s(_O_IK, 64), z(64),
        s(_O_GA, 24), s(_O_IW, 4), z(100),
        s(_O_GM, 2048),
    ]
    out = jnp.concatenate(cols, axis=1)
    assert out.shape[1] == _D_IN_PAD
    return out.astype(BF16)


def _rope_tables(seq):
    half = HEAD_DIM // 2
    inv_freq = ROPE_THETA ** (-jnp.arange(half, dtype=F32) / half)
    ang = jnp.arange(seq, dtype=F32)[:, None] * inv_freq[None, :]
    cos, sin = jnp.cos(ang), jnp.sin(ang)
    cos64 = jnp.concatenate([cos, cos], axis=1)
    sin64 = jnp.concatenate([-sin, sin], axis=1)
    cos_k = jnp.concatenate([cos64, cos64], axis=1)
    sin_k = jnp.concatenate([sin64, sin64], axis=1)
    scale = HEAD_DIM ** -0.5
    q_scale = scale * math.log2(math.e)
    cos_kv = jnp.concatenate([cos64, jnp.ones_like(cos64)], axis=1)
    sin_kv = jnp.concatenate([sin64, jnp.zeros_like(sin64)], axis=1)
    return jnp.stack([cos_k * q_scale, sin_k * q_scale, cos_k * scale, sin_k * scale,
                      cos_k, sin_k, cos_kv, sin_kv], axis=0)


def _proj(x2, gain, w_pad, tabs, seq, tm):
    n = x2.shape[0]
    nblk_seq = seq // tm
    row = lambda width: pl.BlockSpec((tm, width), lambda i: (i, 0))
    full = lambda shape: pl.BlockSpec(shape, lambda i: (0,) * len(shape))
    widths = [(512, BF16), (256, F32), (256, BF16), (256, BF16), (256, BF16), (256, BF16), (512, BF16),
              (128, BF16), (128, BF16), (256, BF16), (128, BF16), (128, F32), (2048, F32)]
    return pl.pallas_call(
        _proj_kernel,
        grid=(n // tm,),
        in_specs=[row(D_MODEL), full((1, D_MODEL)), full((D_MODEL, _D_IN_PAD)),
                  pl.BlockSpec((tabs.shape[0], tm, LANES), lambda i: (0, i % nblk_seq, 0))],
        out_specs=[row(wd) for wd, _ in widths],
        out_shape=[jax.ShapeDtypeStruct((n, wd), dt) for wd, dt in widths],
        compiler_params=pltpu.CompilerParams(dimension_semantics=("arbitrary",), vmem_limit_bytes=VMEM_LIMIT),
    )(x2, gain, w_pad, tabs)


def _compress_kernel(c_ref, pos_ref, w1_ref, w2_ref, out_ref):
    n_chunk = c_ref.shape[3]
    for h in range(NSA_KV_HEADS):
        acc = jnp.zeros((n_chunk, LANES), F32)
        for kv in range(2):
            c = c_ref[0, kv, h]
            a_lo = (c + pos_ref[kv, 0]).astype(BF16)
            a_hi = (c + pos_ref[kv, 1]).astype(BF16)
            half = CMP_STRIDE * HEAD_DIM
            h_lo = _dot(a_lo, w1_ref[kv, :half, :])
            h_hi = _dot(a_hi, w1_ref[kv, half:, :])
            hid = h_lo + pltpu.roll(h_hi, n_chunk - 1, 0)
            act = jax.nn.gelu(hid, approximate=True).astype(BF16)
            acc = acc + _dot(act, w2_ref[kv])
        out_ref[0, h] = acc.astype(out_ref.dtype)


def _compress(c, pos, w1, w2):
    b = c.shape[0]
    n_chunk = c.shape[3]
    return pl.pallas_call(
        _compress_kernel,
        grid=(b,),
        in_specs=[pl.BlockSpec((1,) + c.shape[1:], lambda i: (i, 0, 0, 0, 0)),
                  pl.BlockSpec(pos.shape, lambda i: (0, 0, 0, 0)),
                  pl.BlockSpec(w1.shape, lambda i: (0, 0, 0)),
                  pl.BlockSpec(w2.shape, lambda i: (0, 0, 0))],
        out_specs=pl.BlockSpec((1, NSA_KV_HEADS, n_chunk, LANES), lambda i: (i, 0, 0, 0)),
        out_shape=jax.ShapeDtypeStruct((b, NSA_KV_HEADS, n_chunk, LANES), BF16),
        compiler_params=pltpu.CompilerParams(dimension_semantics=("arbitrary",), vmem_limit_bytes=VMEM_LIMIT),
    )(c, pos, w1, w2)


def _cmp_kernel(qa_ref, kvc_ref, ot_ref, ocmp_ref, selt_ref):
    tq = qa_ref.shape[1]
    n_cmp = kvc_ref.shape[2]
    n_slc = ot_ref.shape[0]
    q0 = pl.program_id(1) * tq
    q = qa_ref[0].astype(F32)
    t3 = q0 + lax.broadcasted_iota(jnp.int32, (NSA_GROUP, tq, n_cmp), 1)
    n3 = lax.broadcasted_iota(jnp.int32, (NSA_GROUP, tq, n_cmp), 2)
    mask = (n3 * CMP_STRIDE + (CMP_BLOCK - 1)) <= t3
    j = lax.broadcasted_iota(jnp.int32, (n_slc, tq), 0)
    cur = (q0 + lax.broadcasted_iota(jnp.int32, (n_slc, tq), 1)) // SLC_BLOCK
    forced = (j == 0) | ((cur - j >= 0) & (cur - j < SLC_LOCAL))
    adm = j <= cur
    outs = []
    for g in range(NSA_KV_HEADS):
        gw = NSA_GROUP * HEAD_DIM
        q4 = _stack_heads(q[:, gw * g:gw * (g + 1)], NSA_GROUP)
        kv = kvc_ref[0, g]
        s = _dot_nt(q4, kv).reshape(NSA_GROUP, tq, n_cmp)
        s = jnp.where(mask, s, NEG_INF)
        m = jnp.max(s, axis=2, keepdims=True)
        e = jnp.where(mask, jnp.exp2(s - m), 0.0)
        l = jnp.sum(e, axis=2, keepdims=True)
        p = e * jnp.where(l > 0.0, 1.0 / l, 0.0)
        o = _dot(p.reshape(NSA_GROUP * tq, n_cmp).astype(BF16), kv)
        outs.append(_unstack_heads(o, NSA_GROUP))
        psum = p[0] + p[1] + p[2] + p[3]
        hi = psum.astype(BF16)
        lo = (psum - hi.astype(F32)).astype(BF16)
        p_slc = _dot_nt(ot_ref[...], hi) + _dot_nt(ot_ref[...], lo)
        blk = jnp.where(forced, BIG, p_slc)
        blk = jnp.where(adm, blk, NEG_INF)
        rank = jnp.zeros((n_slc, tq), F32)
        for i in range(n_slc):
            bi = blk[i:i + 1, :]
            beats = (bi > blk) | ((bi == blk) & (j > i))
            rank = rank + jnp.where(beats, 1.0, 0.0)
        selt_ref[0, g] = jnp.where((rank < float(SLC_TOPN)) & adm, 1.0, 0.0)
    ocmp_ref[0] = jnp.concatenate(outs, axis=1)


def _cmp(qa, kvc, ot, tq):
    b, s, _ = qa.shape
    n_cmp = kvc.shape[2]
    n_slc = ot.shape[0]
    return pl.pallas_call(
        _cmp_kernel,
        grid=(b, s // tq),
        in_specs=[pl.BlockSpec((1, tq, NSA_HEADS * HEAD_DIM), lambda i, j: (i, j, 0)),
                  pl.BlockSpec((1, NSA_KV_HEADS, n_cmp, LANES), lambda i, j: (i, 0, 0, 0)),
                  pl.BlockSpec(ot.shape, lambda i, j: (0, 0))],
        out_specs=[pl.BlockSpec((1, tq, NSA_HEADS * HEAD_DIM), lambda i, j: (i, j, 0)),
                   pl.BlockSpec((1, NSA_KV_HEADS, n_slc, tq), lambda i, j: (i, 0, 0, j))],
        out_shape=[jax.ShapeDtypeStruct((b, s, NSA_HEADS * HEAD_DIM), F32),
                   jax.ShapeDtypeStruct((b, NSA_KV_HEADS, n_slc, s), F32)],
        compiler_params=pltpu.CompilerParams(dimension_semantics=("arbitrary", "arbitrary"),
                                             vmem_limit_bytes=VMEM_LIMIT),
    )(qa, kvc, ot)


def _nsa_kernel(qa_ref, kvs_ref, ovs_ref, kvw_ref, ovw_ref, sel_ref, e_ref, ocmp_ref, misc_ref, gx_ref, o_ref,
                q_ref, *flash):
    tk = flash[0].shape[1]
    tq = qa_ref.shape[1]
    qi = pl.program_id(1)
    q0 = qi * tq
    span = WINDOW + tq
    gsig = _sigmoid(misc_ref[0])
    ocmp = ocmp_ref[0]
    t_s = q0 + lax.broadcasted_iota(jnp.int32, (tq, tk), 0)
    lane_s = lax.broadcasted_iota(jnp.int32, (tq, tk), 1)
    hw = NSA_HEADS * HEAD_DIM
    groups = range(NSA_KV_HEADS)
    cols = [slice(LANES * g, LANES * (g + 1)) for g in groups]
    g_hi = gsig.astype(BF16)
    g_lo = (gsig - g_hi.astype(F32)).astype(BF16)
    gates = _dot(g_hi, gx_ref[...]) + _dot(g_lo, gx_ref[...])

    def gate(branch):
        return gates[:, hw * branch:hw * (branch + 1)]

    q_ref[...] = _stack_heads(qa_ref[0].astype(F32), NSA_HEADS)

    def slc_kv(kt):
        k0 = pl.multiple_of(kt * tk, tk)
        return ([kvs_ref[0, pl.ds(k0, tk), cols[g]] for g in groups],
                [ovs_ref[0, pl.ds(k0, tk), cols[g]] for g in groups])

    def slc_bias(kt, carry):
        causal = (kt * tk + lane_s) <= t_s
        return [jnp.where((_dot(sel_ref[0, g], e_ref[kt]) > 0.5) & causal, 0.0, NEG_INF) for g in groups], carry

    o_slc = _flash_finish(_flash_loop((q0 + tq + tk - 1) // tk, NSA_KV_HEADS, q_ref, slc_kv, slc_bias, 0, flash))

    k_lo = pl.multiple_of(jnp.maximum(q0 + tq - span, 0), tq)
    dist = (q0 + lax.broadcasted_iota(jnp.int32, (tq, span), 0)) - (k_lo + lax.broadcasted_iota(
        jnp.int32, (tq, span), 1))
    bias_w = jnp.where((dist >= 0) & (dist < WINDOW), 0.0, NEG_INF)
    o_win = _flash_finish(_softmax_once(
        q_ref[...], NSA_HEADS, [kvw_ref[0, pl.ds(k_lo, span), cols[g]] for g in groups],
        [ovw_ref[0, pl.ds(k_lo, span), cols[g]] for g in groups], [bias_w] * NSA_KV_HEADS))

    o_ref[0] = (gate(0) * ocmp + gate(1) * o_slc + gate(2) * o_win).astype(o_ref.dtype)


def _nsa(qa, kvs, ovs, kvw, ovw, sel, e, ocmp, misc, tq, tk):
    b, s, _ = qa.shape
    n_slc = sel.shape[3]
    hw = NSA_HEADS * HEAD_DIM
    assert WINDOW % tq == 0 and WINDOW + tq <= s
    gx = np.zeros((LANES, 3 * hw), np.float32)
    for branch in range(3):
        for h in range(NSA_HEADS):
            c0 = branch * hw + HEAD_DIM * h
            gx[_MISC_GA + branch * NSA_HEADS + h, c0:c0 + HEAD_DIM] = 1.0
    gx = jnp.asarray(gx, BF16)
    seq = lambda: pl.BlockSpec((1, s, 2 * LANES), lambda i, j: (i, 0, 0))
    return pl.pallas_call(
        _nsa_kernel,
        grid=(b, s // tq),
        in_specs=[pl.BlockSpec((1, tq, hw), lambda i, j: (i, j, 0)),
                  seq(), seq(), seq(), seq(),
                  pl.BlockSpec((1, NSA_KV_HEADS, tq, n_slc), lambda i, j: (i, 0, j, 0)),
                  pl.BlockSpec(e.shape, lambda i, j: (0, 0, 0)),
                  pl.BlockSpec((1, tq, hw), lambda i, j: (i, j, 0)),
                  pl.BlockSpec((1, tq, LANES), lambda i, j: (i, j, 0)),
                  pl.BlockSpec(gx.shape, lambda i, j: (0, 0))],
        out_specs=pl.BlockSpec((1, tq, hw), lambda i, j: (i, j, 0)),
        out_shape=jax.ShapeDtypeStruct((b, s, hw), BF16),
        scratch_shapes=[pltpu.VMEM((NSA_HEADS * tq, LANES), BF16)] + _flash_scratch(NSA_HEADS, tq, tk),
        compiler_params=pltpu.CompilerParams(dimension_semantics=("arbitrary", "arbitrary"),
                                             vmem_limit_bytes=VMEM_LIMIT),
    )(qa, kvs, ovs, kvw, ovw, sel, e, ocmp, misc, gx)


def _dsa_kernel(qb_ref, kvb_ref, ovb_ref, iq_ref, ik_ref, misc_ref, l2_ref, o_ref,
                score_ref, q_ref, *flash, topk, n_bisect):
    tq = qb_ref.shape[1]
    ts = score_ref.shape[1]
    ta = 2 * ts
    fold = ts // SUBLANES
    qi = pl.program_id(1)
    q0 = qi * tq
    n_pair = (q0 + tq + ta - 1) // ta
    key = lax.broadcasted_iota(jnp.int32, (ts, tq), 0)
    qry = lax.broadcasted_iota(jnp.int32, (ts, tq), 1)

    def fold_min(x):
        return jnp.min(x.reshape(fold, SUBLANES, tq), axis=0)

    def fold_max(x):
        return jnp.max(x.reshape(fold, SUBLANES, tq), axis=0)

    def fold_sum(x):
        return jnp.sum(x.reshape(fold, SUBLANES, tq), axis=0)

    q_ref[0:IDX_HEADS * tq] = _stack_heads(iq_ref[0].astype(F32), IDX_HEADS)
    misc_t = misc_ref[0].T
    wts = [misc_t[_MISC_IW + h:_MISC_IW + h + 1, :] * (IDX_HEADS ** -0.5) for h in range(IDX_HEADS)]

    def logits(c):
        k0 = pl.multiple_of(c * ta, ta)
        return _dot_nt(ik_ref[0, pl.ds(k0, ta), :], q_ref[0:IDX_HEADS * tq])

    def score_body(c, carry):
        mn, mx = carry
        k0 = c * ta
        lg = logits(c)
        for j in range(2):
            rows = slice(j * ts, (j + 1) * ts)
            sc = wts[0] * jnp.maximum(lg[rows, 0:tq], 0.0)
            for h in range(1, IDX_HEADS):
                sc = sc + wts[h] * jnp.maximum(lg[rows, h * tq:(h + 1) * tq], 0.0)
            causal = (k0 + j * ts + key) <= (q0 + qry)
            score_ref[2 * c + j] = jnp.where(causal, sc, NEG_INF)
            mn = jnp.minimum(mn, fold_min(jnp.where(causal, sc, _FMAX)))
            mx = jnp.maximum(mx, fold_max(jnp.where(causal, sc, -_FMAX)))
        return mn, mx

    mn, mx = lax.fori_loop(0, n_pair, score_body, (jnp.full((SUBLANES, tq), _FMAX, F32),
                                                   jnp.full((SUBLANES, tq), -_FMAX, F32)))

    @pl.when(n_pair % 2 == 1)
    def _():
        score_ref[2 * n_pair] = jnp.full((ts, tq), NEG_INF, F32)
        score_ref[2 * n_pair + 1] = jnp.full((ts, tq), NEG_INF, F32)
    lo = jnp.min(mn, axis=0, keepdims=True)
    hi = jnp.max(mx, axis=0, keepdims=True)
    kf = jnp.minimum(q0 + lax.broadcasted_iota(jnp.int32, (1, tq), 1) + 1, topk).astype(F32)

    def midpoint(lo, hi):
        mid = lo + (hi - lo) * 0.5
        return jnp.where(mid >= hi, lo, mid)

    def count_gt(thr):
        def body(c, cnt):
            for j in range(4):
                cnt = cnt + fold_sum(jnp.where(score_ref[4 * c + j] > thr, 1.0, 0.0))
            return cnt
        cnt = lax.fori_loop(0, (n_pair + 1) // 2, body, jnp.zeros((SUBLANES, tq), F32))
        return jnp.sum(cnt, axis=0, keepdims=True)

    def bisect(_, carry):
        lo, hi = carry
        mid = midpoint(lo, hi)
        up = count_gt(mid) >= kf
        return jnp.where(up, mid, lo), jnp.where(up, hi, mid)

    lo, hi = lax.fori_loop(0, n_bisect, bisect, (lo, hi))

    def snap_cond(carry):
        lo, hi = carry
        return jnp.max(hi - lo) > 0.0

    def snap(carry):
        lo, hi = carry
        mid = midpoint(lo, hi)

        def body(c, carry):
            cnt, above, below = carry
            for j in range(2):
                sc = score_ref[2 * c + j]
                gt = sc > mid
                cnt = cnt + fold_sum(jnp.where(gt, 1.0, 0.0))
                above = jnp.minimum(above, fold_min(jnp.where(gt, sc, _FMAX)))
                below = jnp.maximum(below, fold_max(jnp.where(gt, -_FMAX, sc)))
            return cnt, above, below

        cnt, above, below = lax.fori_loop(
            0, n_pair, body, (jnp.zeros((SUBLANES, tq), F32), jnp.full((SUBLANES, tq), _FMAX, F32),
                              jnp.full((SUBLANES, tq), -_FMAX, F32)))
        up = jnp.sum(cnt, axis=0, keepdims=True) >= kf
        return (jnp.where(up, jnp.min(above, axis=0, keepdims=True), lo),
                jnp.where(up, hi, jnp.max(below, axis=0, keepdims=True)))

    v, _ = lax.while_loop(snap_cond, lambda carry: snap(snap(carry)), (lo, hi))
    need = kf - count_gt(v)

    q_ref[...] = _stack_heads(qb_ref[0].astype(F32), DSA_HEADS)

    def att_kv(c):
        k0 = pl.multiple_of(c * ta, ta)
        return [kvb_ref[0, pl.ds(k0, ta), :]], [ovb_ref[0, pl.ds(k0, ta), :]]

    def att_bias(c, run):
        halves = []
        for j in range(2):
            sc = score_ref[2 * c + j]
            eq = sc == v
            pt = _dot(l2_ref[...], jnp.where(eq, 1.0, 0.0).astype(BF16))
            take = (sc > v) | (eq & (run + pt[:ts] < need))
            halves.append(jnp.where(take, 0.0, NEG_INF).T)
            run = run + pt[ts:ts + 1]
        return [jnp.concatenate(halves, axis=1)], run

    accs = _flash_loop(n_pair, 1, q_ref, att_kv, att_bias, jnp.zeros((1, tq), F32), flash)
    o_ref[0] = _flash_finish(accs).astype(o_ref.dtype)


def _dsa(qb, kvb, ovb, iq, ik, misc, l2, tq, topk, n_bisect):
    b, s, _ = qb.shape
    hw = DSA_HEADS * HEAD_DIM
    ts = l2.shape[1]
    assert (s // ts) % 4 == 0 and tq % (2 * ts) == 0 or tq == ts
    return pl.pallas_call(
        functools.partial(_dsa_kernel, topk=topk, n_bisect=n_bisect),
        grid=(b, s // tq),
        in_specs=[pl.BlockSpec((1, tq, hw), lambda i, j: (i, j, 0)),
                  pl.BlockSpec((1, s, LANES), lambda i, j: (i, 0, 0)),
                  pl.BlockSpec((1, s, LANES), lambda i, j: (i, 0, 0)),
                  pl.BlockSpec((1, tq, IDX_HEADS * IDX_DIM), lambda i, j: (i, j, 0)),
                  pl.BlockSpec((1, s, LANES), lambda i, j: (i, 0, 0)),
                  pl.BlockSpec((1, tq, LANES), lambda i, j: (i, j, 0)),
                  pl.BlockSpec(l2.shape, lambda i, j: (0, 0))],
        out_specs=pl.BlockSpec((1, tq, hw), lambda i, j: (i, j, 0)),
        out_shape=jax.ShapeDtypeStruct((b, s, hw), BF16),
        scratch_shapes=[pltpu.VMEM((s // ts, ts, tq), F32),
                        pltpu.VMEM((DSA_HEADS * tq, LANES), BF16)] + _flash_scratch(DSA_HEADS, tq, 2 * ts),
        compiler_params=pltpu.CompilerParams(dimension_semantics=("arbitrary", "arbitrary"),
                                             vmem_limit_bytes=VMEM_LIMIT),
    )(qb, kvb, ovb, iq, ik, misc, l2)


def _post_kernel(on_ref, od_ref, gm_ref, x_ref, wbn_ref, wbd_ref, wo_ref, gffn_ref, wg_ref, wu_ref, wd_ref,
                 gfin_ref, out_ref):
    ya = _dot(on_ref[...], wbn_ref[...])
    yb = _dot(od_ref[...], wbd_ref[...])
    gm = gm_ref[...]
    merged = _sigmoid(gm[:, :D_MODEL]) * ya + _sigmoid(gm[:, D_MODEL:]) * yb
    h = x_ref[...] + _dot(merged.astype(BF16), wo_ref[...])
    ms = jnp.mean(h * h, axis=-1, keepdims=True)
    hn = (h * lax.rsqrt(ms + NORM_EPS) * gffn_ref[...]).astype(BF16)
    acc = jnp.zeros_like(h)
    fc = 2 * LANES
    for c in range(wg_ref.shape[1] // fc):
        cols = slice(c * fc, (c + 1) * fc)
        gt = _dot(hn, wg_ref[:, cols])
        up = _dot(hn, wu_ref[:, cols])
        act = (gt * _sigmoid(gt) * up).astype(BF16)
        acc = acc + _dot(act, wd_ref[cols, :])
    h2 = h + acc
    ms2 = jnp.mean(h2 * h2, axis=-1, keepdims=True)
    out_ref[...] = h2 * lax.rsqrt(ms2 + NORM_EPS) * gfin_ref[...]


def _post(o_nsa, o_dsa, gm, x2, wbn, wbd, wo, gffn, wg, wu, wd, gfin, tm):
    n = x2.shape[0]
    row = lambda width: pl.BlockSpec((tm, width), lambda i: (i, 0))
    full = lambda a: pl.BlockSpec(a.shape, lambda i: (0,) * a.ndim, pipeline_mode=pl.Buffered(1))
    return pl.pallas_call(
        _post_kernel,
        grid=(n // tm,),
        in_specs=[row(o_nsa.shape[1]), row(o_dsa.shape[1]), row(2 * D_MODEL), row(D_MODEL),
                  full(wbn), full(wbd), full(wo), full(gffn), full(wg), full(wu), full(wd), full(gfin)],
        out_specs=row(D_MODEL),
        out_shape=jax.ShapeDtypeStruct((n, D_MODEL), F32),
        compiler_params=pltpu.CompilerParams(dimension_semantics=("arbitrary",), vmem_limit_bytes=VMEM_LIMIT),
    )(o_nsa, o_dsa, gm, x2, wbn, wbd, wo, gffn, wg, wu, wd, gfin)


def _layer(h2, b, s, norm_mix, w_in, cmp_pos_k, cmp_w1_k, cmp_w2_k, cmp_pos_v, cmp_w1_v, cmp_w2_v,
           w_branch_nsa, w_branch_dsa, w_out, norm_ffn, w_gate, w_up, w_down, norm_out):
    tq = LANES
    n_chunk = s // CMP_STRIDE
    n_slc = s // SLC_BLOCK
    tabs = _rope_tables(s)
    qa, kvc, kvs, ovs, kvw, ovw, qb, kvb, ovb, iq, ik, misc, gm = _proj(
        h2, norm_mix[None, :], _relayout_w_in(w_in), tabs, s, 512)
    seq3 = lambda a: a.reshape(b, s, -1)

    c = kvc.reshape(b, n_chunk, CMP_STRIDE, 2, NSA_KV_HEADS, HEAD_DIM).transpose(0, 3, 4, 1, 2, 5)
    c = c.reshape(b, 2, NSA_KV_HEADS, n_chunk, CMP_STRIDE * HEAD_DIM)
    pos = jnp.stack([cmp_pos_k, cmp_pos_v]).reshape(2, 2, 1, CMP_STRIDE * HEAD_DIM)
    w1 = jnp.stack([cmp_w1_k, cmp_w1_v]).astype(BF16)
    zpad = jnp.zeros((CMP_HIDDEN, HEAD_DIM), F32)
    w2 = jnp.stack([jnp.concatenate([cmp_w2_k, zpad], axis=1),
                    jnp.concatenate([zpad, cmp_w2_v], axis=1)]).astype(BF16)
    kvcmp = _compress(c, pos, w1, w2)

    ci = np.arange(n_chunk)[None, :] * CMP_STRIDE
    sj = np.arange(n_slc)[:, None] * SLC_BLOCK
    ot = ((ci < sj + SLC_BLOCK) & (ci + CMP_BLOCK > sj) & (np.arange(n_chunk)[None, :] < n_chunk - 1))
    ocmp, selt = _cmp(seq3(qa), kvcmp, jnp.asarray(ot, BF16), tq)
    sel = selt.transpose(0, 1, 3, 2).astype(BF16)

    tq_att = 2 * LANES
    tk = 256
    e = (np.arange(s)[None, :] // SLC_BLOCK == np.arange(n_slc)[:, None])
    e = jnp.asarray(e.reshape(n_slc, s // tk, tk).transpose(1, 0, 2), BF16)
    o_nsa = _nsa(seq3(qa), seq3(kvs), seq3(ovs), seq3(kvw), seq3(ovw), sel, e, ocmp, seq3(misc), tq_att, tk)

    idx = np.arange(LANES)
    l2 = jnp.asarray(np.concatenate([idx[:, None] > idx[None, :], np.ones((LANES, LANES), bool)], axis=0), BF16)
    o_dsa = _dsa(seq3(qb), seq3(kvb), seq3(ovb), seq3(iq), seq3(ik), seq3(misc), l2, tq_att,
                 min(DSA_TOPK_MAX, s // 4), 12)

    wg, wu, wd = w_gate.astype(BF16), w_up.astype(BF16), w_down.astype(BF16)
    return _post(o_nsa.reshape(b * s, -1), o_dsa.reshape(b * s, -1), gm, h2,
                 w_branch_nsa.astype(BF16), w_branch_dsa.astype(BF16), w_out.astype(BF16),
                 norm_ffn[None, :], wg, wu, wd, norm_out[None, :], 512)


def kernel(x, norm_mix, w_in, cmp_pos_k, cmp_w1_k, cmp_w2_k, cmp_pos_v, cmp_w1_v, cmp_w2_v, w_branch_nsa,
           w_branch_dsa, w_out, norm_ffn, w_gate, w_up, w_down, norm_final):
    b, s, d = x.shape
    depth = norm_mix.shape[0]
    assert depth == 1, "the fused epilogue applies the final norm right after the single layer"
    out = _layer(x.reshape(b * s, d), b, s, norm_mix[0], w_in[0], cmp_pos_k[0], cmp_w1_k[0], cmp_w2_k[0],
                 cmp_pos_v[0], cmp_w1_v[0], cmp_w2_v[0], w_branch_nsa[0], w_branch_dsa[0], w_out[0],
                 norm_ffn[0], w_gate[0], w_up[0], w_down[0], norm_final)
    return out.reshape(b, s, d)
```

```python
import functools
import math

import numpy as np
import jax
import jax.numpy as jnp
from jax import lax
from jax.experimental import pallas as pl
from jax.experimental.pallas import tpu as pltpu

F32 = jnp.float32
BF16 = jnp.bfloat16

D_MODEL = 1024
HEAD_DIM = 64
ROPE_THETA = 10000.0
NORM_EPS = 1e-6
NEG_INF = -1e30
BIG = 1e4
_FMAX = 3.0e38
NSA_HEADS = 8
NSA_KV_HEADS = 2
NSA_GROUP = NSA_HEADS // NSA_KV_HEADS
CMP_BLOCK = 32
CMP_STRIDE = 16
CMP_HIDDEN = 256
SLC_BLOCK = 64
SLC_TOPN = 8
SLC_LOCAL = 2
WINDOW = 512
DSA_HEADS = 8
IDX_HEADS = 4
IDX_DIM = 64
DSA_TOPK_MAX = 256
D_FF = -(-8 * D_MODEL // (3 * 256)) * 256

LANES = 128
SUBLANES = 8
VMEM_LIMIT = 56 * 1024 * 1024

_O_QA, _O_KC, _O_VC, _O_KSL, _O_VSL, _O_KWN, _O_VWN = 0, 512, 640, 768, 896, 1024, 1152
_O_GA, _O_QB, _O_KB, _O_VB, _O_IQ, _O_IK, _O_IW, _O_GM = 1280, 1304, 1816, 1880, 1944, 2200, 2264, 2268
_D_IN = 4316
_N_QA, _N_KVC, _N_KVS, _N_KVW, _N_QB, _N_KVB, _N_IK, _N_IQ, _N_MISC, _N_GM = (
    0, 512, 768, 1024, 1280, 1792, 1920, 2048, 2304, 2432)
_D_IN_PAD = 4480
_MISC_GA, _MISC_IW = 0, 24


def _dot(a, b):
    return jnp.dot(a, b, preferred_element_type=F32)


def _dot_nt(a, b):
    return lax.dot_general(a, b, (((1,), (1,)), ((), ())), preferred_element_type=F32)


def _sigmoid(x):
    return 1.0 / (1.0 + jnp.exp(-x))


def _stack_heads(x, n_heads):
    tq = x.shape[0]
    low = lax.broadcasted_iota(jnp.int32, (tq, LANES), 1) < HEAD_DIM
    parts = []
    for p in range(n_heads // 2):
        slab = x[:, LANES * p:LANES * (p + 1)]
        parts.append(jnp.where(low, slab, 0.0))
        parts.append(jnp.where(low, pltpu.roll(slab, HEAD_DIM, 1), 0.0))
    return jnp.concatenate(parts, axis=0).astype(BF16)


def _unstack_heads(acc, n_heads):
    tq = acc.shape[0] // n_heads
    low = lax.broadcasted_iota(jnp.int32, (tq, LANES), 1) < HEAD_DIM
    outs = []
    for p in range(n_heads // 2):
        a0 = acc[(2 * p) * tq:(2 * p + 1) * tq]
        a1 = acc[(2 * p + 1) * tq:(2 * p + 2) * tq]
        outs.append(jnp.where(low, pltpu.roll(a0, HEAD_DIM, 1), a1))
    return jnp.concatenate(outs, axis=1)


def _softmax_once(q_all, n_heads, kvs, ovs, biases):
    per_group = n_heads // len(kvs)
    tq = q_all.shape[0] // n_heads
    out = []
    for j, (kv, ov, bias) in enumerate(zip(kvs, ovs, biases)):
        s_all = _dot_nt(q_all[j * per_group * tq:(j + 1) * per_group * tq], kv)
        p = []
        for r in range(per_group):
            s = s_all[r * tq:(r + 1) * tq] + bias
            p.append(jnp.exp2(s - jnp.max(s, axis=1, keepdims=True)).astype(BF16))
        pv = _dot(jnp.concatenate(p, axis=0), ov)
        out.extend(pv[r * tq:(r + 1) * tq] for r in range(per_group))
    return out


def _flash_loop(n_chunks, n_groups, q_ref, kv_at, bias_at, carry, scratch):
    s_ref, mx_ref, p_ref, m_ref, acc_ref = scratch
    n_heads, tq, _ = m_ref.shape
    tk = s_ref.shape[1]
    per_group = n_heads // n_groups
    grp_rows = [slice(j * per_group * tq, (j + 1) * per_group * tq) for j in range(n_groups)]
    head_rows = [slice(h * tq, (h + 1) * tq) for h in range(n_heads)]

    def scores(c, carry):
        kvs, _ = kv_at(c)
        biases, carry = bias_at(c, carry)
        out = []
        for j in range(n_groups):
            s_all = _dot_nt(q_ref[grp_rows[j]], kvs[j])
            for r in range(per_group):
                s = s_all[r * tq:(r + 1) * tq] + biases[j]
                out.append((s, jnp.broadcast_to(jnp.max(s, axis=1, keepdims=True), (tq, LANES))))
        return out, carry

    def put_scores(s):
        for h in range(n_heads):
            s_ref[head_rows[h]] = s[h][0]
            mx_ref[h] = s[h][1]

    def values(c, p, acc):
        _, ovs = kv_at(c)
        out = []
        for j in range(n_groups):
            pv = _dot(p[j], ovs[j])
            out.extend(acc[j * per_group + r] + pv[r * tq:(r + 1) * tq] for r in range(per_group))
        return out

    m_ref[...] = jnp.full(m_ref.shape, NEG_INF, F32)
    acc_ref[...] = jnp.zeros(acc_ref.shape, F32)
    p_ref[...] = jnp.zeros(p_ref.shape, BF16)
    s0, carry = scores(0, carry)
    put_scores(s0)

    def body(c, carry):
        acc_new = values(jnp.maximum(c - 1, 0), [p_ref[grp_rows[j]] for j in range(n_groups)],
                         [acc_ref[h] for h in range(n_heads)])
        m_new, p_new = [], []
        for h in range(n_heads):
            d = jnp.minimum(m_ref[h] - mx_ref[h], 0.0)
            m_h = m_ref[h] - d
            acc_new[h] = acc_new[h] * jnp.exp2(d)
            m_wide = m_h if tk == LANES else jnp.concatenate([m_h] * (tk // LANES), axis=1)
            p_new.append(jnp.exp2(s_ref[head_rows[h]] - m_wide).astype(BF16))
            m_new.append(m_h)
        s_next, carry = scores(jnp.minimum(c + 1, n_chunks - 1), carry)
        for h in range(n_heads):
            acc_ref[h] = acc_new[h]
            m_ref[h] = m_new[h]
            p_ref[head_rows[h]] = p_new[h]
        put_scores(s_next)
        return carry

    lax.fori_loop(0, n_chunks, body, carry)
    return values(n_chunks - 1, [p_ref[grp_rows[j]] for j in range(n_groups)],
                  [acc_ref[h] for h in range(n_heads)])


def _flash_scratch(n_heads, tq, tk):
    return [pltpu.VMEM((n_heads * tq, tk), F32), pltpu.VMEM((n_heads, tq, LANES), F32),
            pltpu.VMEM((n_heads * tq, tk), BF16), pltpu.VMEM((n_heads, tq, LANES), F32),
            pltpu.VMEM((n_heads, tq, LANES), F32)]


def _flash_finish(accs):
    tq = accs[0].shape[0]
    low = lax.broadcasted_iota(jnp.int32, (tq, LANES), 1) < HEAD_DIM
    outs = []
    for p in range(len(accs) // 2):
        a0, a1 = accs[2 * p], accs[2 * p + 1]
        num = jnp.where(low, pltpu.roll(a0, HEAD_DIM, 1), a1)
        den = jnp.where(low, a0, pltpu.roll(a1, HEAD_DIM, 1))
        outs.append(num / den)
    return jnp.concatenate(outs, axis=1)


_ROPE_Q, _ROPE_IQ, _ROPE_K, _ROPE_KV = 0, 1, 2, 3


def _proj_kernel(x_ref, g_ref, w_ref, tab_ref, qa_ref, ck0_ref, ck1_ref, cv0_ref, cv1_ref, kvs_ref, ovs_ref,
                 kvw_ref, ovw_ref, qb_ref, kvb_ref, ovb_ref, iq_ref, ik_ref, misc_ref, gm_ref, kc_ref, vc_ref):
    x = x_ref[...]
    tm = x.shape[0]
    ms = jnp.mean(x * x, axis=-1, keepdims=True)
    xn = (x * lax.rsqrt(ms + NORM_EPS) * g_ref[...]).astype(BF16)
    lane = lax.broadcasted_iota(jnp.int32, (tm, LANES), 1)
    low32 = (lane & (HEAD_DIM // 2)) == 0
    low64 = lane < HEAD_DIM

    def rope(z, kind):
        cos = tab_ref[2 * kind]
        sin = tab_ref[2 * kind + 1]
        rot = jnp.where(low32, pltpu.roll(z, LANES - HEAD_DIM // 2, 1), pltpu.roll(z, HEAD_DIM // 2, 1))
        return z * cos + rot * sin

    def emit_slabs(col, slabs):
        z = _dot(xn, w_ref[:, col:col + LANES * len(slabs)])
        for j, (out_ref, out_col, kind, ones_ref) in enumerate(slabs):
            zj = z[:, LANES * j:LANES * (j + 1)]
            if kind is not None:
                zj = rope(zj, kind)
            out_ref[:, out_col:out_col + LANES] = zj.astype(out_ref.dtype)
            if ones_ref is not None:
                ones_ref[:, out_col:out_col + LANES] = jnp.where(low64, 1.0, zj).astype(ones_ref.dtype)

    def emit(col, width, out_ref, out_col, kinds, ones_ref=None):
        emit_slabs(col, [(out_ref, out_col + LANES * j, kinds[j], ones_ref) for j in range(width // LANES)])

    emit(_N_QA, 512, qa_ref, 0, [_ROPE_Q] * 4)
    emit_slabs(_N_KVC, [(kc_ref, 0, _ROPE_K, None), (vc_ref, 0, None, None)])
    n_ck = tm // CMP_STRIDE
    low64c = lax.broadcasted_iota(jnp.int32, (n_ck, LANES), 1) < HEAD_DIM
    for src_ref, c_h0, c_h1 in ((kc_ref, ck0_ref, ck1_ref), (vc_ref, cv0_ref, cv1_ref)):
        for i in range(CMP_STRIDE // 2):
            ta = src_ref[pl.ds(2 * i, n_ck, stride=CMP_STRIDE), :]
            tb = src_ref[pl.ds(2 * i + 1, n_ck, stride=CMP_STRIDE), :]
            cols = slice(LANES * i, LANES * (i + 1))
            c_h0[:, cols] = jnp.where(low64c, ta, pltpu.roll(tb, HEAD_DIM, 1))
            c_h1[:, cols] = jnp.where(low64c, pltpu.roll(ta, HEAD_DIM, 1), tb)
    emit(_N_KVS, 256, kvs_ref, 0, [_ROPE_KV] * 2, ovs_ref)
    emit(_N_KVW, 256, kvw_ref, 0, [_ROPE_KV] * 2, ovw_ref)
    emit(_N_QB, 512, qb_ref, 0, [_ROPE_Q] * 4)
    emit_slabs(_N_KVB, [(kvb_ref, 0, _ROPE_KV, ovb_ref), (ik_ref, 0, _ROPE_K, None)])
    emit(_N_IQ, 256, iq_ref, 0, [_ROPE_IQ] * 2)
    emit(_N_MISC, 128, misc_ref, 0, [None])
    for c in range(4):
        emit(_N_GM + 512 * c, 512, gm_ref, 512 * c, [None] * 4)


def _relayout_w_in(w):
    z = lambda n: jnp.zeros((w.shape[0], n), w.dtype)
    s = lambda a, n: w[:, a:a + n]
    cols = [
        s(_O_QA, 512), s(_O_KC, 128), s(_O_VC, 128),
        s(_O_KSL, 64), s(_O_VSL, 64), s(_O_KSL + 64, 64), s(_O_VSL + 64, 64),
        s(_O_KWN, 64), s(_O_VWN, 64), s(_O_KWN + 64, 64), s(_O_VWN + 64, 64),
        s(_O_QB, 512), s(_O_KB, 64), s(_O_VB, 64),
        s(_O_IK, 64), z(64), s(_O_IQ, 256),
        s(_O_GA, 24), s(_O_IW, 4), z(100),
        s(_O_GM, 2048),
    ]
    out = jnp.concatenate(cols, axis=1)
    assert out.shape[1] == _D_IN_PAD
    return out.astype(BF16)


def _rope_tables(seq):
    half = HEAD_DIM // 2
    inv_freq = ROPE_THETA ** (-jnp.arange(half, dtype=F32) / half)
    ang = jnp.arange(seq, dtype=F32)[:, None] * inv_freq[None, :]
    cos, sin = jnp.cos(ang), jnp.sin(ang)
    cos64 = jnp.concatenate([cos, cos], axis=1)
    sin64 = jnp.concatenate([-sin, sin], axis=1)
    cos_k = jnp.concatenate([cos64, cos64], axis=1)
    sin_k = jnp.concatenate([sin64, sin64], axis=1)
    scale = HEAD_DIM ** -0.5
    q_scale = scale * math.log2(math.e)
    cos_kv = jnp.concatenate([cos64, jnp.ones_like(cos64)], axis=1)
    sin_kv = jnp.concatenate([sin64, jnp.zeros_like(sin64)], axis=1)
    return jnp.stack([cos_k * q_scale, sin_k * q_scale, cos_k * scale, sin_k * scale,
                      cos_k, sin_k, cos_kv, sin_kv], axis=0)


def _proj(x2, gain, w_pad, tabs, seq, tm):
    n = x2.shape[0]
    nblk_seq = seq // tm
    row = lambda width: pl.BlockSpec((tm, width), lambda i: (i, 0))
    full = lambda shape: pl.BlockSpec(shape, lambda i: (0,) * len(shape))
    outs = [(1, 512, BF16)] + [(CMP_STRIDE, CMP_STRIDE * HEAD_DIM, F32)] * 4 + [
        (1, 256, BF16), (1, 256, BF16), (1, 256, BF16), (1, 256, BF16), (1, 512, BF16),
        (1, 128, BF16), (1, 128, BF16), (1, 256, BF16), (1, 128, BF16), (1, 128, F32), (1, 2048, F32)]
    return pl.pallas_call(
        _proj_kernel,
        grid=(n // tm,),
        in_specs=[row(D_MODEL), full((1, D_MODEL)), full((D_MODEL, _D_IN_PAD)),
                  pl.BlockSpec((tabs.shape[0], tm, LANES), lambda i: (0, i % nblk_seq, 0))],
        out_specs=[pl.BlockSpec((tm // div, wd), lambda i: (i, 0)) for div, wd, _ in outs],
        out_shape=[jax.ShapeDtypeStruct((n // div, wd), dt) for div, wd, dt in outs],
        scratch_shapes=[pltpu.VMEM((tm, LANES), F32), pltpu.VMEM((tm, LANES), F32)],
        compiler_params=pltpu.CompilerParams(dimension_semantics=("arbitrary",), vmem_limit_bytes=VMEM_LIMIT),
    )(x2, gain, w_pad, tabs)


def _compress_kernel(ck0_ref, ck1_ref, cv0_ref, cv1_ref, pos_ref, w1_ref, w2_ref, out_ref):
    c_refs = ((ck0_ref, ck1_ref), (cv0_ref, cv1_ref))
    n_chunk = ck0_ref.shape[0]
    for h in range(NSA_KV_HEADS):
        acc = jnp.zeros((n_chunk, LANES), F32)
        for kv in range(2):
            c = c_refs[kv][h][...]
            a_lo = (c + pos_ref[kv, 0]).astype(BF16)
            a_hi = (c + pos_ref[kv, 1]).astype(BF16)
            half = CMP_STRIDE * HEAD_DIM
            h_lo = _dot(a_lo, w1_ref[kv, :half, :])
            h_hi = _dot(a_hi, w1_ref[kv, half:, :])
            hid = h_lo + pltpu.roll(h_hi, n_chunk - 1, 0)
            act = jax.nn.gelu(hid, approximate=True).astype(BF16)
            acc = acc + _dot(act, w2_ref[kv])
        out_ref[0, h] = acc.astype(out_ref.dtype)


def _compress(cs, b, pos, w1, w2):
    n_chunk = cs[0].shape[0] // b
    return pl.pallas_call(
        _compress_kernel,
        grid=(b,),
        in_specs=[pl.BlockSpec((n_chunk, c.shape[1]), lambda i: (i, 0)) for c in cs] + [
                  pl.BlockSpec(pos.shape, lambda i: (0, 0, 0, 0)),
                  pl.BlockSpec(w1.shape, lambda i: (0, 0, 0)),
                  pl.BlockSpec(w2.shape, lambda i: (0, 0, 0))],
        out_specs=pl.BlockSpec((1, NSA_KV_HEADS, n_chunk, LANES), lambda i: (i, 0, 0, 0)),
        out_shape=jax.ShapeDtypeStruct((b, NSA_KV_HEADS, n_chunk, LANES), BF16),
        compiler_params=pltpu.CompilerParams(dimension_semantics=("arbitrary",), vmem_limit_bytes=VMEM_LIMIT),
    )(*cs, pos, w1, w2)


def _cmp_kernel(qa_ref, kvc_ref, ot_ref, ocmp_ref, selt_ref):
    tq = qa_ref.shape[1]
    n_cmp = kvc_ref.shape[2]
    n_slc = ot_ref.shape[0]
    q0 = pl.program_id(1) * tq
    q = qa_ref[0].astype(F32)
    t3 = q0 + lax.broadcasted_iota(jnp.int32, (NSA_GROUP, tq, n_cmp), 1)
    n3 = lax.broadcasted_iota(jnp.int32, (NSA_GROUP, tq, n_cmp), 2)
    mask = (n3 * CMP_STRIDE + (CMP_BLOCK - 1)) <= t3
    j = lax.broadcasted_iota(jnp.int32, (n_slc, tq), 0)
    cur = (q0 + lax.broadcasted_iota(jnp.int32, (n_slc, tq), 1)) // SLC_BLOCK
    forced = (j == 0) | ((cur - j >= 0) & (cur - j < SLC_LOCAL))
    adm = j <= cur
    outs = []
    for g in range(NSA_KV_HEADS):
        gw = NSA_GROUP * HEAD_DIM
        q4 = _stack_heads(q[:, gw * g:gw * (g + 1)], NSA_GROUP)
        kv = kvc_ref[0, g]
        s = _dot_nt(q4, kv).reshape(NSA_GROUP, tq, n_cmp)
        s = jnp.where(mask, s, NEG_INF)
        m = jnp.max(s, axis=2, keepdims=True)
        e = jnp.where(mask, jnp.exp2(s - m), 0.0)
        l = jnp.sum(e, axis=2, keepdims=True)
        p = e * jnp.where(l > 0.0, 1.0 / l, 0.0)
        o = _dot(p.reshape(NSA_GROUP * tq, n_cmp).astype(BF16), kv)
        outs.append(_unstack_heads(o, NSA_GROUP))
        psum = p[0] + p[1] + p[2] + p[3]
        hi = psum.astype(BF16)
        lo = (psum - hi.astype(F32)).astype(BF16)
        p_slc = _dot_nt(ot_ref[...], hi) + _dot_nt(ot_ref[...], lo)
        blk = jnp.where(forced, BIG, p_slc)
        blk = jnp.where(adm, blk, NEG_INF)
        rank = jnp.zeros((n_slc, tq), F32)
        for i in range(n_slc):
            bi = blk[i:i + 1, :]
            beats = (bi > blk) | ((bi == blk) & (j > i))
            rank = rank + jnp.where(beats, 1.0, 0.0)
        selt_ref[0, g] = jnp.where((rank < float(SLC_TOPN)) & adm, 1.0, 0.0)
    ocmp_ref[0] = jnp.concatenate(outs, axis=1)


def _cmp(qa, kvc, ot, tq):
    b, s, _ = qa.shape
    n_cmp = kvc.shape[2]
    n_slc = ot.shape[0]
    return pl.pallas_call(
        _cmp_kernel,
        grid=(b, s // tq),
        in_specs=[pl.BlockSpec((1, tq, NSA_HEADS * HEAD_DIM), lambda i, j: (i, j, 0)),
                  pl.BlockSpec((1, NSA_KV_HEADS, n_cmp, LANES), lambda i, j: (i, 0, 0, 0)),
                  pl.BlockSpec(ot.shape, lambda i, j: (0, 0))],
        out_specs=[pl.BlockSpec((1, tq, NSA_HEADS * HEAD_DIM), lambda i, j: (i, j, 0)),
                   pl.BlockSpec((1, NSA_KV_HEADS, n_slc, tq), lambda i, j: (i, 0, 0, j))],
        out_shape=[jax.ShapeDtypeStruct((b, s, NSA_HEADS * HEAD_DIM), F32),
                   jax.ShapeDtypeStruct((b, NSA_KV_HEADS, n_slc, s), F32)],
        compiler_params=pltpu.CompilerParams(dimension_semantics=("arbitrary", "arbitrary"),
                                             vmem_limit_bytes=VMEM_LIMIT),
    )(qa, kvc, ot)


def _nsa_kernel(qa_ref, kvs_ref, ovs_ref, kvw_ref, ovw_ref, sel_ref, e_ref, ocmp_ref, misc_ref, gx_ref, o_ref,
                q_ref, *flash):
    tk = flash[0].shape[1]
    tq = qa_ref.shape[1]
    qi = pl.program_id(1)
    q0 = qi * tq
    span = WINDOW + tq
    gsig = _sigmoid(misc_ref[0])
    ocmp = ocmp_ref[0]
    t_s = q0 + lax.broadcasted_iota(jnp.int32, (tq, tk), 0)
    lane_s = lax.broadcasted_iota(jnp.int32, (tq, tk), 1)
    hw = NSA_HEADS * HEAD_DIM
    groups = range(NSA_KV_HEADS)
    cols = [slice(LANES * g, LANES * (g + 1)) for g in groups]
    g_hi = gsig.astype(BF16)
    g_lo = (gsig - g_hi.astype(F32)).astype(BF16)
    gates = _dot(g_hi, gx_ref[...]) + _dot(g_lo, gx_ref[...])

    def gate(branch):
        return gates[:, hw * branch:hw * (branch + 1)]

    q_ref[...] = _stack_heads(qa_ref[0].astype(F32), NSA_HEADS)

    def slc_kv(kt):
        k0 = pl.multiple_of(kt * tk, tk)
        return ([kvs_ref[0, pl.ds(k0, tk), cols[g]] for g in groups],
                [ovs_ref[0, pl.ds(k0, tk), cols[g]] for g in groups])

    def slc_bias(kt, carry):
        causal = (kt * tk + lane_s) <= t_s
        return [jnp.where((_dot(sel_ref[0, g], e_ref[kt]) > 0.5) & causal, 0.0, NEG_INF) for g in groups], carry

    o_slc = _flash_finish(_flash_loop((q0 + tq + tk - 1) // tk, NSA_KV_HEADS, q_ref, slc_kv, slc_bias, 0, flash))

    k_lo = pl.multiple_of(jnp.maximum(q0 + tq - span, 0), tq)
    dist = (q0 + lax.broadcasted_iota(jnp.int32, (tq, span), 0)) - (k_lo + lax.broadcasted_iota(
        jnp.int32, (tq, span), 1))
    bias_w = jnp.where((dist >= 0) & (dist < WINDOW), 0.0, NEG_INF)
    o_win = _flash_finish(_softmax_once(
        q_ref[...], NSA_HEADS, [kvw_ref[0, pl.ds(k_lo, span), cols[g]] for g in groups],
        [ovw_ref[0, pl.ds(k_lo, span), cols[g]] for g in groups], [bias_w] * NSA_KV_HEADS))

    o_ref[0] = (gate(0) * ocmp + gate(1) * o_slc + gate(2) * o_win).astype(o_ref.dtype)


def _nsa(qa, kvs, ovs, kvw, ovw, sel, e, ocmp, misc, tq, tk):
    b, s, _ = qa.shape
    n_slc = sel.shape[3]
    hw = NSA_HEADS * HEAD_DIM
    assert WINDOW % tq == 0 and WINDOW + tq <= s
    gx = np.zeros((LANES, 3 * hw), np.float32)
    for branch in range(3):
        for h in range(NSA_HEADS):
            c0 = branch * hw + HEAD_DIM * h
            gx[_MISC_GA + branch * NSA_HEADS + h, c0:c0 + HEAD_DIM] = 1.0
    gx = jnp.asarray(gx, BF16)
    seq = lambda: pl.BlockSpec((1, s, 2 * LANES), lambda i, j: (i, 0, 0))
    return pl.pallas_call(
        _nsa_kernel,
        grid=(b, s // tq),
        in_specs=[pl.BlockSpec((1, tq, hw), lambda i, j: (i, j, 0)),
                  seq(), seq(), seq(), seq(),
                  pl.BlockSpec((1, NSA_KV_HEADS, tq, n_slc), lambda i, j: (i, 0, j, 0)),
                  pl.BlockSpec(e.shape, lambda i, j: (0, 0, 0)),
                  pl.BlockSpec((1, tq, hw), lambda i, j: (i, j, 0)),
                  pl.BlockSpec((1, tq, LANES), lambda i, j: (i, j, 0)),
                  pl.BlockSpec(gx.shape, lambda i, j: (0, 0))],
        out_specs=pl.BlockSpec((1, tq, hw), lambda i, j: (i, j, 0)),
        out_shape=jax.ShapeDtypeStruct((b, s, hw), BF16),
        scratch_shapes=[pltpu.VMEM((NSA_HEADS * tq, LANES), BF16)] + _flash_scratch(NSA_HEADS, tq, tk),
        compiler_params=pltpu.CompilerParams(dimension_semantics=("arbitrary", "arbitrary"),
                                             vmem_limit_bytes=VMEM_LIMIT),
    )(qa, kvs, ovs, kvw, ovw, sel, e, ocmp, misc, gx)


def _dsa_kernel(qb_ref, kvb_ref, ovb_ref, iq_ref, ik_ref, misc_ref, l2_ref, o_ref,
                score_ref, q_ref, *flash, topk, n_bisect):
    tq = qb_ref.shape[1]
    ts = score_ref.shape[1]
    ta = 2 * ts
    fold = ts // SUBLANES
    qi = pl.program_id(1)
    q0 = qi * tq
    n_pair = (q0 + tq + ta - 1) // ta
    key = lax.broadcasted_iota(jnp.int32, (ts, tq), 0)
    qry = lax.broadcasted_iota(jnp.int32, (ts, tq), 1)

    def fold_min(x):
        return jnp.min(x.reshape(fold, SUBLANES, tq), axis=0)

    def fold_max(x):
        return jnp.max(x.reshape(fold, SUBLANES, tq), axis=0)

    def fold_sum(x):
        return jnp.sum(x.reshape(fold, SUBLANES, tq), axis=0)

    q_ref[0:IDX_HEADS * tq] = _stack_heads(iq_ref[0].astype(F32), IDX_HEADS)
    misc_t = misc_ref[0].T
    wts = [misc_t[_MISC_IW + h:_MISC_IW + h + 1, :] * (IDX_HEADS ** -0.5) for h in range(IDX_HEADS)]

    def logits(c):
        k0 = pl.multiple_of(c * ta, ta)
        return _dot_nt(ik_ref[0, pl.ds(k0, ta), :], q_ref[0:IDX_HEADS * tq])

    def score_body(c, carry):
        mn, mx = carry
        k0 = c * ta
        lg = logits(c)
        for j in range(2):
            rows = slice(j * ts, (j + 1) * ts)
            sc = wts[0] * jnp.maximum(lg[rows, 0:tq], 0.0)
            for h in range(1, IDX_HEADS):
                sc = sc + wts[h] * jnp.maximum(lg[rows, h * tq:(h + 1) * tq], 0.0)
            causal = (k0 + j * ts + key) <= (q0 + qry)
            score_ref[2 * c + j] = jnp.where(causal, sc, NEG_INF)
            mn = jnp.minimum(mn, fold_min(jnp.where(causal, sc, _FMAX)))
            mx = jnp.maximum(mx, fold_max(jnp.where(causal, sc, -_FMAX)))
        return mn, mx

    mn, mx = lax.fori_loop(0, n_pair, score_body, (jnp.full((SUBLANES, tq), _FMAX, F32),
                                                   jnp.full((SUBLANES, tq), -_FMAX, F32)))

    @pl.when(n_pair % 2 == 1)
    def _():
        score_ref[2 * n_pair] = jnp.full((ts, tq), NEG_INF, F32)
        score_ref[2 * n_pair + 1] = jnp.full((ts, tq), NEG_INF, F32)
    lo = jnp.min(mn, axis=0, keepdims=True)
    hi = jnp.max(mx, axis=0, keepdims=True)
    kf = jnp.minimum(q0 + lax.broadcasted_iota(jnp.int32, (1, tq), 1) + 1, topk).astype(F32)

    def midpoint(lo, hi):
        mid = lo + (hi - lo) * 0.5
        return jnp.where(mid >= hi, lo, mid)

    def count_gt(thr):
        def body(c, cnt):
            for j in range(4):
                cnt = cnt + fold_sum(jnp.where(score_ref[4 * c + j] > thr, 1.0, 0.0))
            return cnt
        cnt = lax.fori_loop(0, (n_pair + 1) // 2, body, jnp.zeros((SUBLANES, tq), F32))
        return jnp.sum(cnt, axis=0, keepdims=True)

    def bisect(_, carry):
        lo, hi = carry
        mid = midpoint(lo, hi)
        up = count_gt(mid) >= kf
        return jnp.where(up, mid, lo), jnp.where(up, hi, mid)

    lo, hi = lax.fori_loop(0, n_bisect, bisect, (lo, hi))

    def snap_cond(carry):
        lo, hi = carry
        return jnp.max(hi - lo) > 0.0

    def snap(carry):
        lo, hi = carry
        mid = midpoint(lo, hi)

        def body(c, carry):
            cnt, above, below = carry
            for j in range(2):
                sc = score_ref[2 * c + j]
                gt = sc > mid
                cnt = cnt + fold_sum(jnp.where(gt, 1.0, 0.0))
                above = jnp.minimum(above, fold_min(jnp.where(gt, sc, _FMAX)))
                below = jnp.maximum(below, fold_max(jnp.where(gt, -_FMAX, sc)))
            return cnt, above, below

        cnt, above, below = lax.fori_loop(
            0, n_pair, body, (jnp.zeros((SUBLANES, tq), F32), jnp.full((SUBLANES, tq), _FMAX, F32),
                              jnp.full((SUBLANES, tq), -_FMAX, F32)))
        up = jnp.sum(cnt, axis=0, keepdims=True) >= kf
        return (jnp.where(up, jnp.min(above, axis=0, keepdims=True), lo),
                jnp.where(up, hi, jnp.max(below, axis=0, keepdims=True)))

    v, _ = lax.while_loop(snap_cond, lambda carry: snap(snap(carry)), (lo, hi))
    need = kf - count_gt(v)

    q_ref[...] = _stack_heads(qb_ref[0].astype(F32), DSA_HEADS)

    def att_kv(c):
        k0 = pl.multiple_of(c * ta, ta)
        return [kvb_ref[0, pl.ds(k0, ta), :]], [ovb_ref[0, pl.ds(k0, ta), :]]

    def att_bias(c, run):
        halves = []
        for j in range(2):
            sc = score_ref[2 * c + j]
            eq = sc == v
            pt = _dot(l2_ref[...], jnp.where(eq, 1.0, 0.0).astype(BF16))
            take = (sc > v) | (eq & (run + pt[:ts] < need))
            halves.append(jnp.where(take, 0.0, NEG_INF).T)
            run = run + pt[ts:ts + 1]
        return [jnp.concatenate(halves, axis=1)], run

    accs = _flash_loop(n_pair, 1, q_ref, att_kv, att_bias, jnp.zeros((1, tq), F32), flash)
    o_ref[0] = _flash_finish(accs).astype(o_ref.dtype)


def _dsa(qb, kvb, ovb, iq, ik, misc, l2, tq, topk, n_bisect):
    b, s, _ = qb.shape
    hw = DSA_HEADS * HEAD_DIM
    ts = l2.shape[1]
    assert (s // ts) % 4 == 0 and tq % (2 * ts) == 0 or tq == ts
    return pl.pallas_call(
        functools.partial(_dsa_kernel, topk=topk, n_bisect=n_bisect),
        grid=(b, s // tq),
        in_specs=[pl.BlockSpec((1, tq, hw), lambda i, j: (i, j, 0)),
                  pl.BlockSpec((1, s, LANES), lambda i, j: (i, 0, 0)),
                  pl.BlockSpec((1, s, LANES), lambda i, j: (i, 0, 0)),
                  pl.BlockSpec((1, tq, IDX_HEADS * IDX_DIM), lambda i, j: (i, j, 0)),
                  pl.BlockSpec((1, s, LANES), lambda i, j: (i, 0, 0)),
                  pl.BlockSpec((1, tq, LANES), lambda i, j: (i, j, 0)),
                  pl.BlockSpec(l2.shape, lambda i, j: (0, 0))],
        out_specs=pl.BlockSpec((1, tq, hw), lambda i, j: (i, j, 0)),
        out_shape=jax.ShapeDtypeStruct((b, s, hw), BF16),
        scratch_shapes=[pltpu.VMEM((s // ts, ts, tq), F32),
                        pltpu.VMEM((DSA_HEADS * tq, LANES), BF16)] + _flash_scratch(DSA_HEADS, tq, 2 * ts),
        compiler_params=pltpu.CompilerParams(dimension_semantics=("arbitrary", "arbitrary"),
                                             vmem_limit_bytes=VMEM_LIMIT),
    )(qb, kvb, ovb, iq, ik, misc, l2)


def _post_kernel(on_ref, od_ref, gm_ref, x_ref, wbn_ref, wbd_ref, wo_ref, gffn_ref, wg_ref, wu_ref, wd_ref,
                 gfin_ref, out_ref):
    ya = _dot(on_ref[...], wbn_ref[...])
    yb = _dot(od_ref[...], wbd_ref[...])
    gm = gm_ref[...]
    merged = _sigmoid(gm[:, :D_MODEL]) * ya + _sigmoid(gm[:, D_MODEL:]) * yb
    h = x_ref[...] + _dot(merged.astype(BF16), wo_ref[...])
    ms = jnp.mean(h * h, axis=-1, keepdims=True)
    hn = (h * lax.rsqrt(ms + NORM_EPS) * gffn_ref[...]).astype(BF16)
    acc = jnp.zeros_like(h)
    fc = 2 * LANES
    for c in range(wg_ref.shape[1] // fc):
        cols = slice(c * fc, (c + 1) * fc)
        gt = _dot(hn, wg_ref[:, cols])
        up = _dot(hn, wu_ref[:, cols])
        act = (gt * _sigmoid(gt) * up).astype(BF16)
        acc = acc + _dot(act, wd_ref[cols, :])
    h2 = h + acc
    ms2 = jnp.mean(h2 * h2, axis=-1, keepdims=True)
    out_ref[...] = h2 * lax.rsqrt(ms2 + NORM_EPS) * gfin_ref[...]


def _post(o_nsa, o_dsa, gm, x2, wbn, wbd, wo, gffn, wg, wu, wd, gfin, tm):
    n = x2.shape[0]
    row = lambda width: pl.BlockSpec((tm, width), lambda i: (i, 0))
    full = lambda a: pl.BlockSpec(a.shape, lambda i: (0,) * a.ndim, pipeline_mode=pl.Buffered(1))
    return pl.pallas_call(
        _post_kernel,
        grid=(n // tm,),
        in_specs=[row(o_nsa.shape[1]), row(o_dsa.shape[1]), row(2 * D_MODEL), row(D_MODEL),
                  full(wbn), full(wbd), full(wo), full(gffn), full(wg), full(wu), full(wd), full(gfin)],
        out_specs=row(D_MODEL),
        out_shape=jax.ShapeDtypeStruct((n, D_MODEL), F32),
        compiler_params=pltpu.CompilerParams(dimension_semantics=("arbitrary",), vmem_limit_bytes=VMEM_LIMIT),
    )(o_nsa, o_dsa, gm, x2, wbn, wbd, wo, gffn, wg, wu, wd, gfin)


def _layer(h2, b, s, norm_mix, w_in, cmp_pos_k, cmp_w1_k, cmp_w2_k, cmp_pos_v, cmp_w1_v, cmp_w2_v,
           w_branch_nsa, w_branch_dsa, w_out, norm_ffn, w_gate, w_up, w_down, norm_out):
    tq = LANES
    n_chunk = s // CMP_STRIDE
    n_slc = s // SLC_BLOCK
    tabs = _rope_tables(s)
    qa, ck0, ck1, cv0, cv1, kvs, ovs, kvw, ovw, qb, kvb, ovb, iq, ik, misc, gm = _proj(
        h2, norm_mix[None, :], _relayout_w_in(w_in), tabs, s, 512)
    seq3 = lambda a: a.reshape(b, s, -1)

    pos = jnp.stack([cmp_pos_k, cmp_pos_v]).reshape(2, 2, 1, CMP_STRIDE * HEAD_DIM)
    w1 = jnp.stack([cmp_w1_k, cmp_w1_v]).astype(BF16)
    zpad = jnp.zeros((CMP_HIDDEN, HEAD_DIM), F32)
    w2 = jnp.stack([jnp.concatenate([cmp_w2_k, zpad], axis=1),
                    jnp.concatenate([zpad, cmp_w2_v], axis=1)]).astype(BF16)
    kvcmp = _compress((ck0, ck1, cv0, cv1), b, pos, w1, w2)

    ci = np.arange(n_chunk)[None, :] * CMP_STRIDE
    sj = np.arange(n_slc)[:, None] * SLC_BLOCK
    ot = ((ci < sj + SLC_BLOCK) & (ci + CMP_BLOCK > sj) & (np.arange(n_chunk)[None, :] < n_chunk - 1))
    ocmp, selt = _cmp(seq3(qa), kvcmp, jnp.asarray(ot, BF16), tq)
    sel = selt.transpose(0, 1, 3, 2).astype(BF16)

    tq_att = 2 * LANES
    tk = 256
    e = (np.arange(s)[None, :] // SLC_BLOCK == np.arange(n_slc)[:, None])
    e = jnp.asarray(e.reshape(n_slc, s // tk, tk).transpose(1, 0, 2), BF16)
    o_nsa = _nsa(seq3(qa), seq3(kvs), seq3(ovs), seq3(kvw), seq3(ovw), sel, e, ocmp, seq3(misc), tq_att, tk)

    idx = np.arange(LANES)
    l2 = jnp.asarray(np.concatenate([idx[:, None] > idx[None, :], np.ones((LANES, LANES), bool)], axis=0), BF16)
    o_dsa = _dsa(seq3(qb), seq3(kvb), seq3(ovb), seq3(iq), seq3(ik), seq3(misc), l2, tq_att,
                 min(DSA_TOPK_MAX, s // 4), 12)

    wg, wu, wd = w_gate.astype(BF16), w_up.astype(BF16), w_down.astype(BF16)
    return _post(o_nsa.reshape(b * s, -1), o_dsa.reshape(b * s, -1), gm, h2,
                 w_branch_nsa.astype(BF16), w_branch_dsa.astype(BF16), w_out.astype(BF16),
                 norm_ffn[None, :], wg, wu, wd, norm_out[None, :], 512)


def kernel(x, norm_mix, w_in, cmp_pos_k, cmp_w1_k, cmp_w2_k, cmp_pos_v, cmp_w1_v, cmp_w2_v, w_branch_nsa,
           w_branch_dsa, w_out, norm_ffn, w_gate, w_up, w_down, norm_final):
    b, s, d = x.shape
    depth = norm_mix.shape[0]
    assert depth == 1, "the fused epilogue applies the final norm right after the single layer"
    out = _layer(x.reshape(b * s, d), b, s, norm_mix[0], w_in[0], cmp_pos_k[0], cmp_w1_k[0], cmp_w2_k[0],
                 cmp_pos_v[0], cmp_w1_v[0], cmp_w2_v[0], w_branch_nsa[0], w_branch_dsa[0], w_out[0],
                 norm_ffn[0], w_gate[0], w_up[0], w_down[0], norm_final)
    return out.reshape(b, s, d)
```

```python
import functools
import math

import numpy as np
import jax
import jax.numpy as jnp
from jax import lax
from jax.experimental import pallas as pl
from jax.experimental.pallas import tpu as pltpu

F32 = jnp.float32
BF16 = jnp.bfloat16

D_MODEL = 1024
HEAD_DIM = 64
ROPE_THETA = 10000.0
NORM_EPS = 1e-6
NEG_INF = -1e30
BIG = 1e4
_FMAX = 3.0e38
NSA_HEADS = 8
NSA_KV_HEADS = 2
NSA_GROUP = NSA_HEADS // NSA_KV_HEADS
CMP_BLOCK = 32
CMP_STRIDE = 16
CMP_HIDDEN = 256
SLC_BLOCK = 64
SLC_TOPN = 8
SLC_LOCAL = 2
WINDOW = 512
DSA_HEADS = 8
IDX_HEADS = 4
IDX_DIM = 64
DSA_TOPK_MAX = 256
D_FF = -(-8 * D_MODEL // (3 * 256)) * 256

LANES = 128
SUBLANES = 8
VMEM_LIMIT = 56 * 1024 * 1024

_O_QA, _O_KC, _O_VC, _O_KSL, _O_VSL, _O_KWN, _O_VWN = 0, 512, 640, 768, 896, 1024, 1152
_O_GA, _O_QB, _O_KB, _O_VB, _O_IQ, _O_IK, _O_IW, _O_GM = 1280, 1304, 1816, 1880, 1944, 2200, 2264, 2268
_D_IN = 4316
_N_QA, _N_KVC, _N_KVS, _N_KVW, _N_QB, _N_KVB, _N_IK, _N_IQ, _N_MISC, _N_GM = (
    0, 512, 768, 1024, 1280, 1792, 1920, 2048, 2304, 2432)
_D_IN_PAD = 4480
_MISC_GA, _MISC_IW = 0, 24


def _dot(a, b):
    return jnp.dot(a, b, preferred_element_type=F32)


def _dot_nt(a, b):
    return lax.dot_general(a, b, (((1,), (1,)), ((), ())), preferred_element_type=F32)


def _sigmoid(x):
    return 1.0 / (1.0 + jnp.exp(-x))


def _stack_heads(x, n_heads):
    tq = x.shape[0]
    low = lax.broadcasted_iota(jnp.int32, (tq, LANES), 1) < HEAD_DIM
    parts = []
    for p in range(n_heads // 2):
        slab = x[:, LANES * p:LANES * (p + 1)]
        parts.append(jnp.where(low, slab, 0.0))
        parts.append(jnp.where(low, pltpu.roll(slab, HEAD_DIM, 1), 0.0))
    return jnp.concatenate(parts, axis=0).astype(BF16)


def _unstack_heads(acc, n_heads):
    tq = acc.shape[0] // n_heads
    low = lax.broadcasted_iota(jnp.int32, (tq, LANES), 1) < HEAD_DIM
    outs = []
    for p in range(n_heads // 2):
        a0 = acc[(2 * p) * tq:(2 * p + 1) * tq]
        a1 = acc[(2 * p + 1) * tq:(2 * p + 2) * tq]
        outs.append(jnp.where(low, pltpu.roll(a0, HEAD_DIM, 1), a1))
    return jnp.concatenate(outs, axis=1)


def _softmax_once(q_all, n_heads, kvs, ovs, biases):
    per_group = n_heads // len(kvs)
    tq = q_all.shape[0] // n_heads
    out = []
    for j, (kv, ov, bias) in enumerate(zip(kvs, ovs, biases)):
        s_all = _dot_nt(q_all[j * per_group * tq:(j + 1) * per_group * tq], kv)
        p = []
        for r in range(per_group):
            s = s_all[r * tq:(r + 1) * tq] + bias
            p.append(jnp.exp2(s - jnp.max(s, axis=1, keepdims=True)).astype(BF16))
        pv = _dot(jnp.concatenate(p, axis=0), ov)
        out.extend(pv[r * tq:(r + 1) * tq] for r in range(per_group))
    return out


def _flash_loop(n_chunks, n_groups, q_ref, kv_at, bias_at, carry, scratch):
    s_ref, mx_ref, p_ref, m_ref, acc_ref = scratch
    n_heads, tq, _ = m_ref.shape
    tk = s_ref.shape[1]
    per_group = n_heads // n_groups
    grp_rows = [slice(j * per_group * tq, (j + 1) * per_group * tq) for j in range(n_groups)]
    head_rows = [slice(h * tq, (h + 1) * tq) for h in range(n_heads)]

    def scores(c, carry):
        kvs, _ = kv_at(c)
        biases, carry = bias_at(c, carry)
        out = []
        for j in range(n_groups):
            s_all = _dot_nt(q_ref[grp_rows[j]], kvs[j])
            for r in range(per_group):
                s = s_all[r * tq:(r + 1) * tq] + biases[j]
                out.append((s, jnp.broadcast_to(jnp.max(s, axis=1, keepdims=True), (tq, LANES))))
        return out, carry

    def put_scores(s):
        for h in range(n_heads):
            s_ref[head_rows[h]] = s[h][0]
            mx_ref[h] = s[h][1]

    def values(c, p, acc):
        _, ovs = kv_at(c)
        out = []
        for j in range(n_groups):
            pv = _dot(p[j], ovs[j])
            out.extend(acc[j * per_group + r] + pv[r * tq:(r + 1) * tq] for r in range(per_group))
        return out

    m_ref[...] = jnp.full(m_ref.shape, NEG_INF, F32)
    acc_ref[...] = jnp.zeros(acc_ref.shape, F32)
    p_ref[...] = jnp.zeros(p_ref.shape, BF16)
    s0, carry = scores(0, carry)
    put_scores(s0)

    def body(c, carry):
        acc_new = values(jnp.maximum(c - 1, 0), [p_ref[grp_rows[j]] for j in range(n_groups)],
                         [acc_ref[h] for h in range(n_heads)])
        m_new, p_new = [], []
        for h in range(n_heads):
            d = jnp.minimum(m_ref[h] - mx_ref[h], 0.0)
            m_h = m_ref[h] - d
            acc_new[h] = acc_new[h] * jnp.exp2(d)
            m_wide = m_h if tk == LANES else jnp.concatenate([m_h] * (tk // LANES), axis=1)
            p_new.append(jnp.exp2(s_ref[head_rows[h]] - m_wide).astype(BF16))
            m_new.append(m_h)
        s_next, carry = scores(jnp.minimum(c + 1, n_chunks - 1), carry)
        for h in range(n_heads):
            acc_ref[h] = acc_new[h]
            m_ref[h] = m_new[h]
            p_ref[head_rows[h]] = p_new[h]
        put_scores(s_next)
        return carry

    lax.fori_loop(0, n_chunks, body, carry)
    return values(n_chunks - 1, [p_ref[grp_rows[j]] for j in range(n_groups)],
                  [acc_ref[h] for h in range(n_heads)])


def _flash_scratch(n_heads, tq, tk):
    return [pltpu.VMEM((n_heads * tq, tk), F32), pltpu.VMEM((n_heads, tq, LANES), F32),
            pltpu.VMEM((n_heads * tq, tk), BF16), pltpu.VMEM((n_heads, tq, LANES), F32),
            pltpu.VMEM((n_heads, tq, LANES), F32)]


def _flash_finish(accs):
    tq = accs[0].shape[0]
    low = lax.broadcasted_iota(jnp.int32, (tq, LANES), 1) < HEAD_DIM
    outs = []
    for p in range(len(accs) // 2):
        a0, a1 = accs[2 * p], accs[2 * p + 1]
        num = jnp.where(low, pltpu.roll(a0, HEAD_DIM, 1), a1)
        den = jnp.where(low, a0, pltpu.roll(a1, HEAD_DIM, 1))
        outs.append(num / den)
    return jnp.concatenate(outs, axis=1)


_ROPE_Q, _ROPE_IQ, _ROPE_K, _ROPE_KV = 0, 1, 2, 3


def _proj_kernel(x_ref, g_ref, w_ref, tab_ref, qa_ref, ck0_ref, ck1_ref, cv0_ref, cv1_ref, kvs_ref, ovs_ref,
                 kvw_ref, ovw_ref, qb_ref, kvb_ref, ovb_ref, iq_ref, ik_ref, misc_ref, gm_ref, kc_ref, vc_ref):
    x = x_ref[...]
    tm = x.shape[0]
    ms = jnp.mean(x * x, axis=-1, keepdims=True)
    xn = (x * lax.rsqrt(ms + NORM_EPS) * g_ref[...]).astype(BF16)
    lane = lax.broadcasted_iota(jnp.int32, (tm, LANES), 1)
    low32 = (lane & (HEAD_DIM // 2)) == 0
    low64 = lane < HEAD_DIM

    def rope(z, kind):
        cos = tab_ref[2 * kind]
        sin = tab_ref[2 * kind + 1]
        rot = jnp.where(low32, pltpu.roll(z, LANES - HEAD_DIM // 2, 1), pltpu.roll(z, HEAD_DIM // 2, 1))
        return z * cos + rot * sin

    def emit_slabs(col, slabs):
        z = _dot(xn, w_ref[:, col:col + LANES * len(slabs)])
        for j, (out_ref, out_col, kind, ones_ref) in enumerate(slabs):
            zj = z[:, LANES * j:LANES * (j + 1)]
            if kind is not None:
                zj = rope(zj, kind)
            out_ref[:, out_col:out_col + LANES] = zj.astype(out_ref.dtype)
            if ones_ref is not None:
                ones_ref[:, out_col:out_col + LANES] = jnp.where(low64, 1.0, zj).astype(ones_ref.dtype)

    def emit(col, width, out_ref, out_col, kinds, ones_ref=None):
        emit_slabs(col, [(out_ref, out_col + LANES * j, kinds[j], ones_ref) for j in range(width // LANES)])

    emit(_N_QA, 512, qa_ref, 0, [_ROPE_Q] * 4)
    emit_slabs(_N_KVC, [(kc_ref, 0, _ROPE_K, None), (vc_ref, 0, None, None)])
    n_ck = tm // CMP_STRIDE
    low64c = lax.broadcasted_iota(jnp.int32, (n_ck, LANES), 1) < HEAD_DIM
    for src_ref, c_h0, c_h1 in ((kc_ref, ck0_ref, ck1_ref), (vc_ref, cv0_ref, cv1_ref)):
        for i in range(CMP_STRIDE // 2):
            ta = src_ref[pl.ds(2 * i, n_ck, stride=CMP_STRIDE), :]
            tb = src_ref[pl.ds(2 * i + 1, n_ck, stride=CMP_STRIDE), :]
            cols = slice(LANES * i, LANES * (i + 1))
            c_h0[:, cols] = jnp.where(low64c, ta, pltpu.roll(tb, HEAD_DIM, 1))
            c_h1[:, cols] = jnp.where(low64c, pltpu.roll(ta, HEAD_DIM, 1), tb)
    emit(_N_KVS, 256, kvs_ref, 0, [_ROPE_KV] * 2, ovs_ref)
    emit(_N_KVW, 256, kvw_ref, 0, [_ROPE_KV] * 2, ovw_ref)
    emit(_N_QB, 512, qb_ref, 0, [_ROPE_Q] * 4)
    emit_slabs(_N_KVB, [(kvb_ref, 0, _ROPE_KV, ovb_ref), (ik_ref, 0, _ROPE_K, None)])
    emit(_N_IQ, 256, iq_ref, 0, [_ROPE_IQ] * 2)
    emit(_N_MISC, 128, misc_ref, 0, [None])
    for c in range(4):
        emit(_N_GM + 512 * c, 512, gm_ref, 512 * c, [None] * 4)


def _relayout_w_in(w):
    z = lambda n: jnp.zeros((w.shape[0], n), w.dtype)
    s = lambda a, n: w[:, a:a + n]
    cols = [
        s(_O_QA, 512), s(_O_KC, 128), s(_O_VC, 128),
        s(_O_KSL, 64), s(_O_VSL, 64), s(_O_KSL + 64, 64), s(_O_VSL + 64, 64),
        s(_O_KWN, 64), s(_O_VWN, 64), s(_O_KWN + 64, 64), s(_O_VWN + 64, 64),
        s(_O_QB, 512), s(_O_KB, 64), s(_O_VB, 64),
        s(_O_IK, 64), z(64), s(_O_IQ, 256),
        s(_O_GA, 24), s(_O_IW, 4), z(100),
        s(_O_GM, 2048),
    ]
    out = jnp.concatenate(cols, axis=1)
    assert out.shape[1] == _D_IN_PAD
    return out.astype(BF16)


def _rope_tables(seq):
    half = HEAD_DIM // 2
    inv_freq = ROPE_THETA ** (-jnp.arange(half, dtype=F32) / half)
    ang = jnp.arange(seq, dtype=F32)[:, None] * inv_freq[None, :]
    cos, sin = jnp.cos(ang), jnp.sin(ang)
    cos64 = jnp.concatenate([cos, cos], axis=1)
    sin64 = jnp.concatenate([-sin, sin], axis=1)
    cos_k = jnp.concatenate([cos64, cos64], axis=1)
    sin_k = jnp.concatenate([sin64, sin64], axis=1)
    scale = HEAD_DIM ** -0.5
    q_scale = scale * math.log2(math.e)
    cos_kv = jnp.concatenate([cos64, jnp.ones_like(cos64)], axis=1)
    sin_kv = jnp.concatenate([sin64, jnp.zeros_like(sin64)], axis=1)
    return jnp.stack([cos_k * q_scale, sin_k * q_scale, cos_k * scale, sin_k * scale,
                      cos_k, sin_k, cos_kv, sin_kv], axis=0)


def _proj(x2, gain, w_pad, tabs, seq, tm):
    n = x2.shape[0]
    nblk_seq = seq // tm
    row = lambda width: pl.BlockSpec((tm, width), lambda i: (i, 0))
    full = lambda shape: pl.BlockSpec(shape, lambda i: (0,) * len(shape))
    outs = [(1, 512, BF16)] + [(CMP_STRIDE, CMP_STRIDE * HEAD_DIM, F32)] * 4 + [
        (1, 256, BF16), (1, 256, BF16), (1, 256, BF16), (1, 256, BF16), (1, 512, BF16),
        (1, 128, BF16), (1, 128, BF16), (1, 256, BF16), (1, 128, BF16), (1, 128, F32), (1, 2048, F32)]
    return pl.pallas_call(
        _proj_kernel,
        grid=(n // tm,),
        in_specs=[row(D_MODEL), full((1, D_MODEL)), full((D_MODEL, _D_IN_PAD)),
                  pl.BlockSpec((tabs.shape[0], tm, LANES), lambda i: (0, i % nblk_seq, 0))],
        out_specs=[pl.BlockSpec((tm // div, wd), lambda i: (i, 0)) for div, wd, _ in outs],
        out_shape=[jax.ShapeDtypeStruct((n // div, wd), dt) for div, wd, dt in outs],
        scratch_shapes=[pltpu.VMEM((tm, LANES), F32), pltpu.VMEM((tm, LANES), F32)],
        compiler_params=pltpu.CompilerParams(dimension_semantics=("arbitrary",), vmem_limit_bytes=VMEM_LIMIT),
    )(x2, gain, w_pad, tabs)


def _compress_kernel(ck0_ref, ck1_ref, cv0_ref, cv1_ref, pos_ref, w1_ref, w2_ref, out_ref):
    c_refs = ((ck0_ref, ck1_ref), (cv0_ref, cv1_ref))
    n_chunk = ck0_ref.shape[0]
    for h in range(NSA_KV_HEADS):
        acc = jnp.zeros((n_chunk, LANES), F32)
        for kv in range(2):
            c = c_refs[kv][h][...]
            a_lo = (c + pos_ref[kv, 0]).astype(BF16)
            a_hi = (c + pos_ref[kv, 1]).astype(BF16)
            half = CMP_STRIDE * HEAD_DIM
            h_lo = _dot(a_lo, w1_ref[kv, :half, :])
            h_hi = _dot(a_hi, w1_ref[kv, half:, :])
            hid = h_lo + pltpu.roll(h_hi, n_chunk - 1, 0)
            act = jax.nn.gelu(hid, approximate=True).astype(BF16)
            acc = acc + _dot(act, w2_ref[kv])
        out_ref[0, h] = acc.astype(out_ref.dtype)


def _compress(cs, b, pos, w1, w2):
    n_chunk = cs[0].shape[0] // b
    return pl.pallas_call(
        _compress_kernel,
        grid=(b,),
        in_specs=[pl.BlockSpec((n_chunk, c.shape[1]), lambda i: (i, 0)) for c in cs] + [
                  pl.BlockSpec(pos.shape, lambda i: (0, 0, 0, 0)),
                  pl.BlockSpec(w1.shape, lambda i: (0, 0, 0)),
                  pl.BlockSpec(w2.shape, lambda i: (0, 0, 0))],
        out_specs=pl.BlockSpec((1, NSA_KV_HEADS, n_chunk, LANES), lambda i: (i, 0, 0, 0)),
        out_shape=jax.ShapeDtypeStruct((b, NSA_KV_HEADS, n_chunk, LANES), BF16),
        compiler_params=pltpu.CompilerParams(dimension_semantics=("arbitrary",), vmem_limit_bytes=VMEM_LIMIT),
    )(*cs, pos, w1, w2)


def _cmp_branch(q_ref, kvc_ref, ot_ref, ocmp_ref, sel_ref, q0, tq):
    n_cmp = kvc_ref.shape[2]
    n_slc = ot_ref.shape[0]
    t3 = q0 + lax.broadcasted_iota(jnp.int32, (NSA_GROUP, tq, n_cmp), 1)
    n3 = lax.broadcasted_iota(jnp.int32, (NSA_GROUP, tq, n_cmp), 2)
    mask = (n3 * CMP_STRIDE + (CMP_BLOCK - 1)) <= t3
    j = lax.broadcasted_iota(jnp.int32, (n_slc, tq), 0)
    cur = (q0 + lax.broadcasted_iota(jnp.int32, (n_slc, tq), 1)) // SLC_BLOCK
    forced = (j == 0) | ((cur - j >= 0) & (cur - j < SLC_LOCAL))
    adm = j <= cur
    for g in range(NSA_KV_HEADS):
        gw = NSA_GROUP * HEAD_DIM
        q4 = q_ref[g * NSA_GROUP * tq:(g + 1) * NSA_GROUP * tq]
        kv = kvc_ref[0, g]
        s = _dot_nt(q4, kv).reshape(NSA_GROUP, tq, n_cmp)
        s = jnp.where(mask, s, NEG_INF)
        m = jnp.max(s, axis=2, keepdims=True)
        e = jnp.where(mask, jnp.exp2(s - m), 0.0)
        l = jnp.sum(e, axis=2, keepdims=True)
        p = e * jnp.where(l > 0.0, 1.0 / l, 0.0)
        o = _dot(p.reshape(NSA_GROUP * tq, n_cmp).astype(BF16), kv)
        ocmp_ref[:, gw * g:gw * (g + 1)] = _unstack_heads(o, NSA_GROUP)
        psum = p[0] + p[1] + p[2] + p[3]
        hi = psum.astype(BF16)
        lo = (psum - hi.astype(F32)).astype(BF16)
        p_slc = _dot_nt(ot_ref[...], hi) + _dot_nt(ot_ref[...], lo)
        blk = jnp.where(forced, BIG, p_slc)
        blk = jnp.where(adm, blk, NEG_INF)
        rank = jnp.zeros((n_slc, tq), F32)
        for i in range(n_slc):
            bi = blk[i:i + 1, :]
            beats = (bi > blk) | ((bi == blk) & (j > i))
            rank = rank + jnp.where(beats, 1.0, 0.0)
        selt = jnp.where((rank < float(SLC_TOPN)) & adm, 1.0, 0.0)
        selt = jnp.concatenate([selt, jnp.zeros((LANES - n_slc, tq), F32)], axis=0)
        sel_ref[g] = selt.T.astype(BF16)


def _nsa_kernel(qa_ref, kvc_ref, ot_ref, kvs_ref, ovs_ref, kvw_ref, ovw_ref, e_ref, misc_ref, gx_ref, o_ref,
                q_ref, sel_ref, ocmp_ref, *flash):
    tk = flash[0].shape[1]
    tq = qa_ref.shape[1]
    qi = pl.program_id(1)
    q0 = qi * tq
    span = WINDOW + tq
    gsig = _sigmoid(misc_ref[0])
    t_s = q0 + lax.broadcasted_iota(jnp.int32, (tq, tk), 0)
    lane_s = lax.broadcasted_iota(jnp.int32, (tq, tk), 1)
    hw = NSA_HEADS * HEAD_DIM
    groups = range(NSA_KV_HEADS)
    cols = [slice(LANES * g, LANES * (g + 1)) for g in groups]
    g_hi = gsig.astype(BF16)
    g_lo = (gsig - g_hi.astype(F32)).astype(BF16)
    gates = _dot(g_hi, gx_ref[...]) + _dot(g_lo, gx_ref[...])

    def gate(branch):
        return gates[:, hw * branch:hw * (branch + 1)]

    q_ref[...] = _stack_heads(qa_ref[0].astype(F32), NSA_HEADS)
    _cmp_branch(q_ref, kvc_ref, ot_ref, ocmp_ref, sel_ref, q0, tq)

    def slc_kv(kt):
        k0 = pl.multiple_of(kt * tk, tk)
        return ([kvs_ref[0, pl.ds(k0, tk), cols[g]] for g in groups],
                [ovs_ref[0, pl.ds(k0, tk), cols[g]] for g in groups])

    def slc_bias(kt, carry):
        causal = (kt * tk + lane_s) <= t_s
        return [jnp.where((_dot(sel_ref[g], e_ref[kt]) > 0.5) & causal, 0.0, NEG_INF) for g in groups], carry

    o_slc = _flash_finish(_flash_loop((q0 + tq + tk - 1) // tk, NSA_KV_HEADS, q_ref, slc_kv, slc_bias, 0, flash))

    k_lo = pl.multiple_of(jnp.maximum(q0 + tq - span, 0), tq)
    dist = (q0 + lax.broadcasted_iota(jnp.int32, (tq, span), 0)) - (k_lo + lax.broadcasted_iota(
        jnp.int32, (tq, span), 1))
    bias_w = jnp.where((dist >= 0) & (dist < WINDOW), 0.0, NEG_INF)
    o_win = _flash_finish(_softmax_once(
        q_ref[...], NSA_HEADS, [kvw_ref[0, pl.ds(k_lo, span), cols[g]] for g in groups],
        [ovw_ref[0, pl.ds(k_lo, span), cols[g]] for g in groups], [bias_w] * NSA_KV_HEADS))

    o_ref[0] = (gate(0) * ocmp_ref[...] + gate(1) * o_slc + gate(2) * o_win).astype(o_ref.dtype)


def _nsa(qa, kvc, ot, kvs, ovs, kvw, ovw, e, misc, tq, tk):
    b, s, _ = qa.shape
    n_cmp = kvc.shape[2]
    hw = NSA_HEADS * HEAD_DIM
    assert WINDOW % tq == 0 and WINDOW + tq <= s
    gx = np.zeros((LANES, 3 * hw), np.float32)
    for branch in range(3):
        for h in range(NSA_HEADS):
            c0 = branch * hw + HEAD_DIM * h
            gx[_MISC_GA + branch * NSA_HEADS + h, c0:c0 + HEAD_DIM] = 1.0
    gx = jnp.asarray(gx, BF16)
    seq = lambda: pl.BlockSpec((1, s, 2 * LANES), lambda i, j: (i, 0, 0))
    return pl.pallas_call(
        _nsa_kernel,
        grid=(b, s // tq),
        in_specs=[pl.BlockSpec((1, tq, hw), lambda i, j: (i, j, 0)),
                  pl.BlockSpec((1, NSA_KV_HEADS, n_cmp, LANES), lambda i, j: (i, 0, 0, 0)),
                  pl.BlockSpec(ot.shape, lambda i, j: (0, 0)),
                  seq(), seq(), seq(), seq(),
                  pl.BlockSpec(e.shape, lambda i, j: (0, 0, 0)),
                  pl.BlockSpec((1, tq, LANES), lambda i, j: (i, j, 0)),
                  pl.BlockSpec(gx.shape, lambda i, j: (0, 0))],
        out_specs=pl.BlockSpec((1, tq, hw), lambda i, j: (i, j, 0)),
        out_shape=jax.ShapeDtypeStruct((b, s, hw), BF16),
        scratch_shapes=[pltpu.VMEM((NSA_HEADS * tq, LANES), BF16),
                        pltpu.VMEM((NSA_KV_HEADS, tq, LANES), BF16),
                        pltpu.VMEM((tq, hw), F32)] + _flash_scratch(NSA_HEADS, tq, tk),
        compiler_params=pltpu.CompilerParams(dimension_semantics=("arbitrary", "arbitrary"),
                                             vmem_limit_bytes=VMEM_LIMIT),
    )(qa, kvc, ot, kvs, ovs, kvw, ovw, e, misc, gx)


def _dsa_kernel(qb_ref, kvb_ref, ovb_ref, iq_ref, ik_ref, misc_ref, l2_ref, o_ref,
                score_ref, q_ref, *flash, topk, n_bisect):
    tq = qb_ref.shape[1]
    ts = score_ref.shape[1]
    ta = 2 * ts
    fold = ts // SUBLANES
    qi = pl.program_id(1)
    q0 = qi * tq
    n_pair = (q0 + tq + ta - 1) // ta
    key = lax.broadcasted_iota(jnp.int32, (ts, tq), 0)
    qry = lax.broadcasted_iota(jnp.int32, (ts, tq), 1)

    def fold_min(x):
        return jnp.min(x.reshape(fold, SUBLANES, tq), axis=0)

    def fold_max(x):
        return jnp.max(x.reshape(fold, SUBLANES, tq), axis=0)

    def fold_sum(x):
        return jnp.sum(x.reshape(fold, SUBLANES, tq), axis=0)

    q_ref[0:IDX_HEADS * tq] = _stack_heads(iq_ref[0].astype(F32), IDX_HEADS)
    misc_t = misc_ref[0].T
    wts = [misc_t[_MISC_IW + h:_MISC_IW + h + 1, :] * (IDX_HEADS ** -0.5) for h in range(IDX_HEADS)]

    def logits(c):
        k0 = pl.multiple_of(c * ta, ta)
        return _dot_nt(ik_ref[0, pl.ds(k0, ta), :], q_ref[0:IDX_HEADS * tq])

    def score_body(c, carry):
        mn, mx = carry
        k0 = c * ta
        lg = logits(c)
        for j in range(2):
            rows = slice(j * ts, (j + 1) * ts)
            sc = wts[0] * jnp.maximum(lg[rows, 0:tq], 0.0)
            for h in range(1, IDX_HEADS):
                sc = sc + wts[h] * jnp.maximum(lg[rows, h * tq:(h + 1) * tq], 0.0)
            causal = (k0 + j * ts + key) <= (q0 + qry)
            score_ref[2 * c + j] = jnp.where(causal, sc, NEG_INF)
            mn = jnp.minimum(mn, fold_min(jnp.where(causal, sc, _FMAX)))
            mx = jnp.maximum(mx, fold_max(jnp.where(causal, sc, -_FMAX)))
        return mn, mx

    mn, mx = lax.fori_loop(0, n_pair, score_body, (jnp.full((SUBLANES, tq), _FMAX, F32),
                                                   jnp.full((SUBLANES, tq), -_FMAX, F32)))

    @pl.when(n_pair % 2 == 1)
    def _():
        score_ref[2 * n_pair] = jnp.full((ts, tq), NEG_INF, F32)
        score_ref[2 * n_pair + 1] = jnp.full((ts, tq), NEG_INF, F32)
    lo = jnp.min(mn, axis=0, keepdims=True)
    hi = jnp.max(mx, axis=0, keepdims=True)
    kf = jnp.minimum(q0 + lax.broadcasted_iota(jnp.int32, (1, tq), 1) + 1, topk).astype(F32)

    def midpoint(lo, hi):
        mid = lo + (hi - lo) * 0.5
        return jnp.where(mid >= hi, lo, mid)

    def count_gt(thr):
        def body(c, cnt):
            for j in range(4):
                cnt = cnt + fold_sum(jnp.where(score_ref[4 * c + j] > thr, 1.0, 0.0))
            return cnt
        cnt = lax.fori_loop(0, (n_pair + 1) // 2, body, jnp.zeros((SUBLANES, tq), F32))
        return jnp.sum(cnt, axis=0, keepdims=True)

    def bisect(_, carry):
        lo, hi = carry
        mid = midpoint(lo, hi)
        up = count_gt(mid) >= kf
        return jnp.where(up, mid, lo), jnp.where(up, hi, mid)

    lo, hi = lax.fori_loop(0, n_bisect, bisect, (lo, hi))

    def snap_cond(carry):
        lo, hi = carry
        return jnp.max(hi - lo) > 0.0

    def snap(carry):
        lo, hi = carry
        mid = midpoint(lo, hi)

        def body(c, carry):
            cnt, above, below = carry
            for j in range(2):
                sc = score_ref[2 * c + j]
                gt = sc > mid
                cnt = cnt + fold_sum(jnp.where(gt, 1.0, 0.0))
                above = jnp.minimum(above, fold_min(jnp.where(gt, sc, _FMAX)))
                below = jnp.maximum(below, fold_max(jnp.where(gt, -_FMAX, sc)))
            return cnt, above, below

        cnt, above, below = lax.fori_loop(
            0, n_pair, body, (jnp.zeros((SUBLANES, tq), F32), jnp.full((SUBLANES, tq), _FMAX, F32),
                              jnp.full((SUBLANES, tq), -_FMAX, F32)))
        up = jnp.sum(cnt, axis=0, keepdims=True) >= kf
        return (jnp.where(up, jnp.min(above, axis=0, keepdims=True), lo),
                jnp.where(up, hi, jnp.max(below, axis=0, keepdims=True)))

    v, _ = lax.while_loop(snap_cond, lambda carry: snap(snap(carry)), (lo, hi))
    need = kf - count_gt(v)

    q_ref[...] = _stack_heads(qb_ref[0].astype(F32), DSA_HEADS)

    def att_kv(c):
        k0 = pl.multiple_of(c * ta, ta)
        return [kvb_ref[0, pl.ds(k0, ta), :]], [ovb_ref[0, pl.ds(k0, ta), :]]

    def att_bias(c, run):
        halves = []
        for j in range(2):
            sc = score_ref[2 * c + j]
            eq = sc == v
            pt = _dot(l2_ref[...], jnp.where(eq, 1.0, 0.0).astype(BF16))
            take = (sc > v) | (eq & (run + pt[:ts] < need))
            halves.append(jnp.where(take, 0.0, NEG_INF).T)
            run = run + pt[ts:ts + 1]
        return [jnp.concatenate(halves, axis=1)], run

    accs = _flash_loop(n_pair, 1, q_ref, att_kv, att_bias, jnp.zeros((1, tq), F32), flash)
    o_ref[0] = _flash_finish(accs).astype(o_ref.dtype)


def _dsa(qb, kvb, ovb, iq, ik, misc, l2, tq, topk, n_bisect):
    b, s, _ = qb.shape
    hw = DSA_HEADS * HEAD_DIM
    ts = l2.shape[1]
    assert (s // ts) % 4 == 0 and tq % (2 * ts) == 0 or tq == ts
    return pl.pallas_call(
        functools.partial(_dsa_kernel, topk=topk, n_bisect=n_bisect),
        grid=(b, s // tq),
        in_specs=[pl.BlockSpec((1, tq, hw), lambda i, j: (i, j, 0)),
                  pl.BlockSpec((1, s, LANES), lambda i, j: (i, 0, 0)),
                  pl.BlockSpec((1, s, LANES), lambda i, j: (i, 0, 0)),
                  pl.BlockSpec((1, tq, IDX_HEADS * IDX_DIM), lambda i, j: (i, j, 0)),
                  pl.BlockSpec((1, s, LANES), lambda i, j: (i, 0, 0)),
                  pl.BlockSpec((1, tq, LANES), lambda i, j: (i, j, 0)),
                  pl.BlockSpec(l2.shape, lambda i, j: (0, 0))],
        out_specs=pl.BlockSpec((1, tq, hw), lambda i, j: (i, j, 0)),
        out_shape=jax.ShapeDtypeStruct((b, s, hw), BF16),
        scratch_shapes=[pltpu.VMEM((s // ts, ts, tq), F32),
                        pltpu.VMEM((DSA_HEADS * tq, LANES), BF16)] + _flash_scratch(DSA_HEADS, tq, 2 * ts),
        compiler_params=pltpu.CompilerParams(dimension_semantics=("arbitrary", "arbitrary"),
                                             vmem_limit_bytes=VMEM_LIMIT),
    )(qb, kvb, ovb, iq, ik, misc, l2)


def _post_kernel(on_ref, od_ref, gm_ref, x_ref, wbn_ref, wbd_ref, wo_ref, gffn_ref, wg_ref, wu_ref, wd_ref,
                 gfin_ref, out_ref):
    ya = _dot(on_ref[...], wbn_ref[...])
    yb = _dot(od_ref[...], wbd_ref[...])
    gm = gm_ref[...]
    merged = _sigmoid(gm[:, :D_MODEL]) * ya + _sigmoid(gm[:, D_MODEL:]) * yb
    h = x_ref[...] + _dot(merged.astype(BF16), wo_ref[...])
    ms = jnp.mean(h * h, axis=-1, keepdims=True)
    hn = (h * lax.rsqrt(ms + NORM_EPS) * gffn_ref[...]).astype(BF16)
    acc = jnp.zeros_like(h)
    fc = 2 * LANES
    for c in range(wg_ref.shape[1] // fc):
        cols = slice(c * fc, (c + 1) * fc)
        gt = _dot(hn, wg_ref[:, cols])
        up = _dot(hn, wu_ref[:, cols])
        act = (gt * _sigmoid(gt) * up).astype(BF16)
        acc = acc + _dot(act, wd_ref[cols, :])
    h2 = h + acc
    ms2 = jnp.mean(h2 * h2, axis=-1, keepdims=True)
    out_ref[...] = h2 * lax.rsqrt(ms2 + NORM_EPS) * gfin_ref[...]


def _post(o_nsa, o_dsa, gm, x2, wbn, wbd, wo, gffn, wg, wu, wd, gfin, tm):
    n = x2.shape[0]
    row = lambda width: pl.BlockSpec((tm, width), lambda i: (i, 0))
    full = lambda a: pl.BlockSpec(a.shape, lambda i: (0,) * a.ndim, pipeline_mode=pl.Buffered(1))
    return pl.pallas_call(
        _post_kernel,
        grid=(n // tm,),
        in_specs=[row(o_nsa.shape[1]), row(o_dsa.shape[1]), row(2 * D_MODEL), row(D_MODEL),
                  full(wbn), full(wbd), full(wo), full(gffn), full(wg), full(wu), full(wd), full(gfin)],
        out_specs=row(D_MODEL),
        out_shape=jax.ShapeDtypeStruct((n, D_MODEL), F32),
        compiler_params=pltpu.CompilerParams(dimension_semantics=("arbitrary",), vmem_limit_bytes=VMEM_LIMIT),
    )(o_nsa, o_dsa, gm, x2, wbn, wbd, wo, gffn, wg, wu, wd, gfin)


def _layer(h2, b, s, norm_mix, w_in, cmp_pos_k, cmp_w1_k, cmp_w2_k, cmp_pos_v, cmp_w1_v, cmp_w2_v,
           w_branch_nsa, w_branch_dsa, w_out, norm_ffn, w_gate, w_up, w_down, norm_out):
    tq = LANES
    n_chunk = s // CMP_STRIDE
    n_slc = s // SLC_BLOCK
    tabs = _rope_tables(s)
    qa, ck0, ck1, cv0, cv1, kvs, ovs, kvw, ovw, qb, kvb, ovb, iq, ik, misc, gm = _proj(
        h2, norm_mix[None, :], _relayout_w_in(w_in), tabs, s, 512)
    seq3 = lambda a: a.reshape(b, s, -1)

    pos = jnp.stack([cmp_pos_k, cmp_pos_v]).reshape(2, 2, 1, CMP_STRIDE * HEAD_DIM)
    w1 = jnp.stack([cmp_w1_k, cmp_w1_v]).astype(BF16)
    zpad = jnp.zeros((CMP_HIDDEN, HEAD_DIM), F32)
    w2 = jnp.stack([jnp.concatenate([cmp_w2_k, zpad], axis=1),
                    jnp.concatenate([zpad, cmp_w2_v], axis=1)]).astype(BF16)
    kvcmp = _compress((ck0, ck1, cv0, cv1), b, pos, w1, w2)

    ci = np.arange(n_chunk)[None, :] * CMP_STRIDE
    sj = np.arange(n_slc)[:, None] * SLC_BLOCK
    ot = ((ci < sj + SLC_BLOCK) & (ci + CMP_BLOCK > sj) & (np.arange(n_chunk)[None, :] < n_chunk - 1))

    tq_att = 2 * LANES
    tk = 256
    assert n_slc <= LANES
    e = (np.arange(s)[None, :] // SLC_BLOCK == np.arange(LANES)[:, None])
    e = jnp.asarray(e.reshape(LANES, s // tk, tk).transpose(1, 0, 2), BF16)
    o_nsa = _nsa(seq3(qa), kvcmp, jnp.asarray(ot, BF16), seq3(kvs), seq3(ovs), seq3(kvw), seq3(ovw), e,
                 seq3(misc), tq_att, tk)

    idx = np.arange(LANES)
    l2 = jnp.asarray(np.concatenate([idx[:, None] > idx[None, :], np.ones((LANES, LANES), bool)], axis=0), BF16)
    o_dsa = _dsa(seq3(qb), seq3(kvb), seq3(ovb), seq3(iq), seq3(ik), seq3(misc), l2, tq_att,
                 min(DSA_TOPK_MAX, s // 4), 12)

    wg, wu, wd = w_gate.astype(BF16), w_up.astype(BF16), w_down.astype(BF16)
    return _post(o_nsa.reshape(b * s, -1), o_dsa.reshape(b * s, -1), gm, h2,
                 w_branch_nsa.astype(BF16), w_branch_dsa.astype(BF16), w_out.astype(BF16),
                 norm_ffn[None, :], wg, wu, wd, norm_out[None, :], 512)


def kernel(x, norm_mix, w_in, cmp_pos_k, cmp_w1_k, cmp_w2_k, cmp_pos_v, cmp_w1_v, cmp_w2_v, w_branch_nsa,
           w_branch_dsa, w_out, norm_ffn, w_gate, w_up, w_down, norm_final):
    b, s, d = x.shape
    depth = norm_mix.shape[0]
    assert depth == 1, "the fused epilogue applies the final norm right after the single layer"
    out = _layer(x.reshape(b * s, d), b, s, norm_mix[0], w_in[0], cmp_pos_k[0], cmp_w1_k[0], cmp_w2_k[0],
                 cmp_pos_v[0], cmp_w1_v[0], cmp_w2_v[0], w_branch_nsa[0], w_branch_dsa[0], w_out[0],
                 norm_ffn[0], w_gate[0], w_up[0], w_down[0], norm_final)
    return out.reshape(b, s, d)
```

```python
import functools
import math

import numpy as np
import jax
import jax.numpy as jnp
from jax import lax
from jax.experimental import pallas as pl
from jax.experimental.pallas import tpu as pltpu

F32 = jnp.float32
BF16 = jnp.bfloat16

D_MODEL = 1024
HEAD_DIM = 64
ROPE_THETA = 10000.0
NORM_EPS = 1e-6
NEG_INF = -1e30
BIG = 1e4
_FMAX = 3.0e38
NSA_HEADS = 8
NSA_KV_HEADS = 2
NSA_GROUP = NSA_HEADS // NSA_KV_HEADS
CMP_BLOCK = 32
CMP_STRIDE = 16
CMP_HIDDEN = 256
SLC_BLOCK = 64
SLC_TOPN = 8
SLC_LOCAL = 2
WINDOW = 512
DSA_HEADS = 8
IDX_HEADS = 4
IDX_DIM = 64
DSA_TOPK_MAX = 256
D_FF = -(-8 * D_MODEL // (3 * 256)) * 256

LANES = 128
SUBLANES = 8
VMEM_LIMIT = 56 * 1024 * 1024

_TM = 512
_TQ = 2 * LANES
_TK = 2 * LANES
_N_BISECT = 12

_O_QA, _O_KC, _O_VC, _O_KSL, _O_VSL, _O_KWN, _O_VWN = 0, 512, 640, 768, 896, 1024, 1152
_O_GA, _O_QB, _O_KB, _O_VB, _O_IQ, _O_IK, _O_IW, _O_GM = 1280, 1304, 1816, 1880, 1944, 2200, 2264, 2268
_D_IN = 4316
_N_QA, _N_KVC, _N_KVS, _N_KVW, _N_QB, _N_KVB, _N_IK, _N_IQ, _N_MISC, _N_GM = (
    0, 512, 768, 1024, 1280, 1792, 1920, 2048, 2304, 2432)
_D_IN_PAD = 4480
_MISC_GA, _MISC_IW = 0, 24


def _dot(a, b):
    return jnp.dot(a, b, preferred_element_type=F32)


def _dot_nt(a, b):
    return lax.dot_general(a, b, (((1,), (1,)), ((), ())), preferred_element_type=F32)


def _sigmoid(x):
    return 1.0 / (1.0 + jnp.exp(-x))


def _stack_heads(x, n_heads):
    tq = x.shape[0]
    low = lax.broadcasted_iota(jnp.int32, (tq, LANES), 1) < HEAD_DIM
    parts = []
    for p in range(n_heads // 2):
        slab = x[:, LANES * p:LANES * (p + 1)]
        parts.append(jnp.where(low, slab, 0.0))
        parts.append(jnp.where(low, pltpu.roll(slab, HEAD_DIM, 1), 0.0))
    return jnp.concatenate(parts, axis=0).astype(BF16)


def _unstack_heads(acc, n_heads):
    tq = acc.shape[0] // n_heads
    low = lax.broadcasted_iota(jnp.int32, (tq, LANES), 1) < HEAD_DIM
    outs = []
    for p in range(n_heads // 2):
        a0 = acc[(2 * p) * tq:(2 * p + 1) * tq]
        a1 = acc[(2 * p + 1) * tq:(2 * p + 2) * tq]
        outs.append(jnp.where(low, pltpu.roll(a0, HEAD_DIM, 1), a1))
    return jnp.concatenate(outs, axis=1)


def _softmax_once(q_all, n_heads, kvs, ovs, biases):
    per_group = n_heads // len(kvs)
    tq = q_all.shape[0] // n_heads
    out = []
    for j, (kv, ov, bias) in enumerate(zip(kvs, ovs, biases)):
        s_all = _dot_nt(q_all[j * per_group * tq:(j + 1) * per_group * tq], kv)
        p = []
        for r in range(per_group):
            s = s_all[r * tq:(r + 1) * tq] + bias
            p.append(jnp.exp2(s - jnp.max(s, axis=1, keepdims=True)).astype(BF16))
        pv = _dot(jnp.concatenate(p, axis=0), ov)
        out.extend(pv[r * tq:(r + 1) * tq] for r in range(per_group))
    return out


def _flash_loop(n_chunks, n_groups, q_ref, kv_at, bias_at, carry, scratch):
    s_ref, mx_ref, p_ref, m_ref, acc_ref = scratch
    n_heads, tq, _ = m_ref.shape
    tk = s_ref.shape[1]
    per_group = n_heads // n_groups
    grp_rows = [slice(j * per_group * tq, (j + 1) * per_group * tq) for j in range(n_groups)]
    head_rows = [slice(h * tq, (h + 1) * tq) for h in range(n_heads)]

    def scores(c, carry):
        kvs, _ = kv_at(c)
        biases, carry = bias_at(c, carry)
        out = []
        for j in range(n_groups):
            s_all = _dot_nt(q_ref[grp_rows[j]], kvs[j])
            for r in range(per_group):
                s = s_all[r * tq:(r + 1) * tq] + biases[j]
                out.append((s, jnp.broadcast_to(jnp.max(s, axis=1, keepdims=True), (tq, LANES))))
        return out, carry

    def put_scores(s):
        for h in range(n_heads):
            s_ref[head_rows[h]] = s[h][0]
            mx_ref[h] = s[h][1]

    def values(c, p, acc):
        _, ovs = kv_at(c)
        out = []
        for j in range(n_groups):
            pv = _dot(p[j], ovs[j])
            out.extend(acc[j * per_group + r] + pv[r * tq:(r + 1) * tq] for r in range(per_group))
        return out

    m_ref[...] = jnp.full(m_ref.shape, NEG_INF, F32)
    acc_ref[...] = jnp.zeros(acc_ref.shape, F32)
    p_ref[...] = jnp.zeros(p_ref.shape, BF16)
    s0, carry = scores(0, carry)
    put_scores(s0)

    def body(c, carry):
        acc_new = values(jnp.maximum(c - 1, 0), [p_ref[grp_rows[j]] for j in range(n_groups)],
                         [acc_ref[h] for h in range(n_heads)])
        m_new, p_new = [], []
        for h in range(n_heads):
            d = jnp.minimum(m_ref[h] - mx_ref[h], 0.0)
            m_h = m_ref[h] - d
            acc_new[h] = acc_new[h] * jnp.exp2(d)
            m_wide = m_h if tk == LANES else jnp.concatenate([m_h] * (tk // LANES), axis=1)
            p_new.append(jnp.exp2(s_ref[head_rows[h]] - m_wide).astype(BF16))
            m_new.append(m_h)
        s_next, carry = scores(jnp.minimum(c + 1, n_chunks - 1), carry)
        for h in range(n_heads):
            acc_ref[h] = acc_new[h]
            m_ref[h] = m_new[h]
            p_ref[head_rows[h]] = p_new[h]
        put_scores(s_next)
        return carry

    lax.fori_loop(0, n_chunks, body, carry)
    return values(n_chunks - 1, [p_ref[grp_rows[j]] for j in range(n_groups)],
                  [acc_ref[h] for h in range(n_heads)])


def _flash_scratch(n_heads, tq, tk):
    return [pltpu.VMEM((n_heads * tq, tk), F32), pltpu.VMEM((n_heads, tq, LANES), F32),
            pltpu.VMEM((n_heads * tq, tk), BF16), pltpu.VMEM((n_heads, tq, LANES), F32),
            pltpu.VMEM((n_heads, tq, LANES), F32)]


def _flash_finish(accs):
    tq = accs[0].shape[0]
    low = lax.broadcasted_iota(jnp.int32, (tq, LANES), 1) < HEAD_DIM
    outs = []
    for p in range(len(accs) // 2):
        a0, a1 = accs[2 * p], accs[2 * p + 1]
        num = jnp.where(low, pltpu.roll(a0, HEAD_DIM, 1), a1)
        den = jnp.where(low, a0, pltpu.roll(a1, HEAD_DIM, 1))
        outs.append(num / den)
    return jnp.concatenate(outs, axis=1)


_ROPE_Q, _ROPE_IQ, _ROPE_K, _ROPE_KV = 0, 1, 2, 3


def _proj_kernel(x_ref, g_ref, w_ref, tab_ref, qa_ref, ck0_ref, ck1_ref, cv0_ref, cv1_ref, kvs_ref, ovs_ref,
                 kvw_ref, ovw_ref, qb_ref, kvb_ref, ovb_ref, iq_ref, ik_ref, misc_ref, gm_ref, kc_ref, vc_ref):
    x = x_ref[...]
    tm = x.shape[0]
    ms = jnp.mean(x * x, axis=-1, keepdims=True)
    xn = (x * lax.rsqrt(ms + NORM_EPS) * g_ref[...]).astype(BF16)
    lane = lax.broadcasted_iota(jnp.int32, (tm, LANES), 1)
    low32 = (lane & (HEAD_DIM // 2)) == 0
    low64 = lane < HEAD_DIM

    def rope(z, kind):
        cos = tab_ref[2 * kind]
        sin = tab_ref[2 * kind + 1]
        rot = jnp.where(low32, pltpu.roll(z, LANES - HEAD_DIM // 2, 1), pltpu.roll(z, HEAD_DIM // 2, 1))
        return z * cos + rot * sin

    def emit_slabs(col, slabs):
        z = _dot(xn, w_ref[:, col:col + LANES * len(slabs)])
        for j, (out_ref, out_col, kind, ones_ref) in enumerate(slabs):
            zj = z[:, LANES * j:LANES * (j + 1)]
            if kind is not None:
                zj = rope(zj, kind)
            out_ref[:, out_col:out_col + LANES] = zj.astype(out_ref.dtype)
            if ones_ref is not None:
                ones_ref[:, out_col:out_col + LANES] = jnp.where(low64, 1.0, zj).astype(ones_ref.dtype)

    def emit(col, width, out_ref, out_col, kinds, ones_ref=None):
        emit_slabs(col, [(out_ref, out_col + LANES * j, kinds[j], ones_ref) for j in range(width // LANES)])

    emit(_N_QA, 512, qa_ref, 0, [_ROPE_Q] * 4)
    emit_slabs(_N_KVC, [(kc_ref, 0, _ROPE_K, None), (vc_ref, 0, None, None)])
    n_ck = tm // CMP_STRIDE
    low64c = lax.broadcasted_iota(jnp.int32, (n_ck, LANES), 1) < HEAD_DIM
    for src_ref, c_h0, c_h1 in ((kc_ref, ck0_ref, ck1_ref), (vc_ref, cv0_ref, cv1_ref)):
        for i in range(CMP_STRIDE // 2):
            ta = src_ref[pl.ds(2 * i, n_ck, stride=CMP_STRIDE), :]
            tb = src_ref[pl.ds(2 * i + 1, n_ck, stride=CMP_STRIDE), :]
            cols = slice(LANES * i, LANES * (i + 1))
            c_h0[:, cols] = jnp.where(low64c, ta, pltpu.roll(tb, HEAD_DIM, 1))
            c_h1[:, cols] = jnp.where(low64c, pltpu.roll(ta, HEAD_DIM, 1), tb)
    emit(_N_KVS, 256, kvs_ref, 0, [_ROPE_KV] * 2, ovs_ref)
    emit(_N_KVW, 256, kvw_ref, 0, [_ROPE_KV] * 2, ovw_ref)
    emit(_N_QB, 512, qb_ref, 0, [_ROPE_Q] * 4)
    emit_slabs(_N_KVB, [(kvb_ref, 0, _ROPE_KV, ovb_ref), (ik_ref, 0, _ROPE_K, None)])
    emit(_N_IQ, 256, iq_ref, 0, [_ROPE_IQ] * 2)
    emit(_N_MISC, 128, misc_ref, 0, [None])
    for c in range(4):
        emit(_N_GM + 512 * c, 512, gm_ref, 512 * c, [None] * 4)


def _relayout_w_in(w):
    z = lambda n: jnp.zeros((w.shape[0], n), w.dtype)
    s = lambda a, n: w[:, a:a + n]
    cols = [
        s(_O_QA, 512), s(_O_KC, 128), s(_O_VC, 128),
        s(_O_KSL, 64), s(_O_VSL, 64), s(_O_KSL + 64, 64), s(_O_VSL + 64, 64),
        s(_O_KWN, 64), s(_O_VWN, 64), s(_O_KWN + 64, 64), s(_O_VWN + 64, 64),
        s(_O_QB, 512), s(_O_KB, 64), s(_O_VB, 64),
        s(_O_IK, 64), z(64), s(_O_IQ, 256),
        s(_O_GA, 24), s(_O_IW, 4), z(100),
        s(_O_GM, 2048),
    ]
    out = jnp.concatenate(cols, axis=1)
    assert out.shape[1] == _D_IN_PAD
    return out.astype(BF16)


def _rope_tables(seq):
    half = HEAD_DIM // 2
    inv_freq = ROPE_THETA ** (-jnp.arange(half, dtype=F32) / half)
    ang = jnp.arange(seq, dtype=F32)[:, None] * inv_freq[None, :]
    cos, sin = jnp.cos(ang), jnp.sin(ang)
    cos64 = jnp.concatenate([cos, cos], axis=1)
    sin64 = jnp.concatenate([-sin, sin], axis=1)
    cos_k = jnp.concatenate([cos64, cos64], axis=1)
    sin_k = jnp.concatenate([sin64, sin64], axis=1)
    scale = HEAD_DIM ** -0.5
    q_scale = scale * math.log2(math.e)
    cos_kv = jnp.concatenate([cos64, jnp.ones_like(cos64)], axis=1)
    sin_kv = jnp.concatenate([sin64, jnp.zeros_like(sin64)], axis=1)
    return jnp.stack([cos_k * q_scale, sin_k * q_scale, cos_k * scale, sin_k * scale,
                      cos_k, sin_k, cos_kv, sin_kv], axis=0)


def _proj(x2, gain, w_pad, tabs, seq, tm):
    n = x2.shape[0]
    nblk_seq = seq // tm
    row = lambda width: pl.BlockSpec((tm, width), lambda i: (i, 0))
    full = lambda shape: pl.BlockSpec(shape, lambda i: (0,) * len(shape))
    outs = [(1, 512, BF16)] + [(CMP_STRIDE, CMP_STRIDE * HEAD_DIM, F32)] * 4 + [
        (1, 256, BF16), (1, 256, BF16), (1, 256, BF16), (1, 256, BF16), (1, 512, BF16),
        (1, 128, BF16), (1, 128, BF16), (1, 256, BF16), (1, 128, BF16), (1, 128, F32), (1, 2048, F32)]
    return pl.pallas_call(
        _proj_kernel,
        grid=(n // tm,),
        in_specs=[row(D_MODEL), full((1, D_MODEL)), full((D_MODEL, _D_IN_PAD)),
                  pl.BlockSpec((tabs.shape[0], tm, LANES), lambda i: (0, i % nblk_seq, 0))],
        out_specs=[pl.BlockSpec((tm // div, wd), lambda i: (i, 0)) for div, wd, _ in outs],
        out_shape=[jax.ShapeDtypeStruct((n // div, wd), dt) for div, wd, dt in outs],
        scratch_shapes=[pltpu.VMEM((tm, LANES), F32), pltpu.VMEM((tm, LANES), F32)],
        compiler_params=pltpu.CompilerParams(dimension_semantics=("arbitrary",), vmem_limit_bytes=VMEM_LIMIT),
    )(x2, gain, w_pad, tabs)


def _compress_kernel(ck0_ref, ck1_ref, cv0_ref, cv1_ref, pos_ref, w1_ref, w2_ref, out_ref):
    c_refs = ((ck0_ref, ck1_ref), (cv0_ref, cv1_ref))
    n_chunk = ck0_ref.shape[0]
    for h in range(NSA_KV_HEADS):
        acc = jnp.zeros((n_chunk, LANES), F32)
        for kv in range(2):
            c = c_refs[kv][h][...]
            a_lo = (c + pos_ref[kv, 0]).astype(BF16)
            a_hi = (c + pos_ref[kv, 1]).astype(BF16)
            half = CMP_STRIDE * HEAD_DIM
            h_lo = _dot(a_lo, w1_ref[kv, :half, :])
            h_hi = _dot(a_hi, w1_ref[kv, half:, :])
            hid = h_lo + pltpu.roll(h_hi, n_chunk - 1, 0)
            act = jax.nn.gelu(hid, approximate=True).astype(BF16)
            acc = acc + _dot(act, w2_ref[kv])
        out_ref[0, h] = acc.astype(out_ref.dtype)


def _compress(cs, b, pos, w1, w2):
    n_chunk = cs[0].shape[0] // b
    return pl.pallas_call(
        _compress_kernel,
        grid=(b,),
        in_specs=[pl.BlockSpec((n_chunk, c.shape[1]), lambda i: (i, 0)) for c in cs] + [
                  pl.BlockSpec(pos.shape, lambda i: (0, 0, 0, 0)),
                  pl.BlockSpec(w1.shape, lambda i: (0, 0, 0)),
                  pl.BlockSpec(w2.shape, lambda i: (0, 0, 0))],
        out_specs=pl.BlockSpec((1, NSA_KV_HEADS, n_chunk, LANES), lambda i: (i, 0, 0, 0)),
        out_shape=jax.ShapeDtypeStruct((b, NSA_KV_HEADS, n_chunk, LANES), BF16),
        compiler_params=pltpu.CompilerParams(dimension_semantics=("arbitrary",), vmem_limit_bytes=VMEM_LIMIT),
    )(*cs, pos, w1, w2)


def _cmp_branch(q_ref, kvc_ref, ot_ref, ocmp_ref, sel_ref, q0, tq):
    n_cmp = kvc_ref.shape[2]
    n_slc = ot_ref.shape[0]
    t3 = q0 + lax.broadcasted_iota(jnp.int32, (NSA_GROUP, tq, n_cmp), 1)
    n3 = lax.broadcasted_iota(jnp.int32, (NSA_GROUP, tq, n_cmp), 2)
    mask = (n3 * CMP_STRIDE + (CMP_BLOCK - 1)) <= t3
    j = lax.broadcasted_iota(jnp.int32, (n_slc, tq), 0)
    cur = (q0 + lax.broadcasted_iota(jnp.int32, (n_slc, tq), 1)) // SLC_BLOCK
    forced = (j == 0) | ((cur - j >= 0) & (cur - j < SLC_LOCAL))
    adm = j <= cur
    for g in range(NSA_KV_HEADS):
        gw = NSA_GROUP * HEAD_DIM
        q4 = q_ref[g * NSA_GROUP * tq:(g + 1) * NSA_GROUP * tq]
        kv = kvc_ref[0, g]
        s = _dot_nt(q4, kv).reshape(NSA_GROUP, tq, n_cmp)
        s = jnp.where(mask, s, NEG_INF)
        m = jnp.max(s, axis=2, keepdims=True)
        e = jnp.where(mask, jnp.exp2(s - m), 0.0)
        l = jnp.sum(e, axis=2, keepdims=True)
        p = e * jnp.where(l > 0.0, 1.0 / l, 0.0)
        o = _dot(p.reshape(NSA_GROUP * tq, n_cmp).astype(BF16), kv)
        ocmp_ref[:, gw * g:gw * (g + 1)] = _unstack_heads(o, NSA_GROUP)
        psum = p[0] + p[1] + p[2] + p[3]
        hi = psum.astype(BF16)
        lo = (psum - hi.astype(F32)).astype(BF16)
        p_slc = _dot_nt(ot_ref[...], hi) + _dot_nt(ot_ref[...], lo)
        blk = jnp.where(forced, BIG, p_slc)
        blk = jnp.where(adm, blk, NEG_INF)
        rank = jnp.zeros((n_slc, tq), F32)
        for i in range(n_slc):
            bi = blk[i:i + 1, :]
            beats = (bi > blk) | ((bi == blk) & (j > i))
            rank = rank + jnp.where(beats, 1.0, 0.0)
        selt = jnp.where((rank < float(SLC_TOPN)) & adm, 1.0, 0.0)
        selt = jnp.concatenate([selt, jnp.zeros((LANES - n_slc, tq), F32)], axis=0)
        sel_ref[g] = selt.T.astype(BF16)


def _nsa_kernel(qa_ref, kvc_ref, ot_ref, kvs_ref, ovs_ref, kvw_ref, ovw_ref, e_ref, misc_ref, gx_ref, o_ref,
                q_ref, sel_ref, ocmp_ref, *flash):
    tk = flash[0].shape[1]
    tq = qa_ref.shape[1]
    qi = pl.program_id(1)
    q0 = qi * tq
    span = WINDOW + tq
    gsig = _sigmoid(misc_ref[0])
    t_s = q0 + lax.broadcasted_iota(jnp.int32, (tq, tk), 0)
    lane_s = lax.broadcasted_iota(jnp.int32, (tq, tk), 1)
    hw = NSA_HEADS * HEAD_DIM
    groups = range(NSA_KV_HEADS)
    cols = [slice(LANES * g, LANES * (g + 1)) for g in groups]
    g_hi = gsig.astype(BF16)
    g_lo = (gsig - g_hi.astype(F32)).astype(BF16)
    gates = _dot(g_hi, gx_ref[...]) + _dot(g_lo, gx_ref[...])

    def gate(branch):
        return gates[:, hw * branch:hw * (branch + 1)]

    q_ref[...] = _stack_heads(qa_ref[0].astype(F32), NSA_HEADS)
    _cmp_branch(q_ref, kvc_ref, ot_ref, ocmp_ref, sel_ref, q0, tq)

    def slc_kv(kt):
        k0 = pl.multiple_of(kt * tk, tk)
        return ([kvs_ref[0, pl.ds(k0, tk), cols[g]] for g in groups],
                [ovs_ref[0, pl.ds(k0, tk), cols[g]] for g in groups])

    def slc_bias(kt, carry):
        causal = (kt * tk + lane_s) <= t_s
        return [jnp.where((_dot(sel_ref[g], e_ref[kt]) > 0.5) & causal, 0.0, NEG_INF) for g in groups], carry

    o_slc = _flash_finish(_flash_loop((q0 + tq + tk - 1) // tk, NSA_KV_HEADS, q_ref, slc_kv, slc_bias, 0, flash))

    k_lo = pl.multiple_of(jnp.maximum(q0 + tq - span, 0), tq)
    dist = (q0 + lax.broadcasted_iota(jnp.int32, (tq, span), 0)) - (k_lo + lax.broadcasted_iota(
        jnp.int32, (tq, span), 1))
    bias_w = jnp.where((dist >= 0) & (dist < WINDOW), 0.0, NEG_INF)
    o_win = _flash_finish(_softmax_once(
        q_ref[...], NSA_HEADS, [kvw_ref[0, pl.ds(k_lo, span), cols[g]] for g in groups],
        [ovw_ref[0, pl.ds(k_lo, span), cols[g]] for g in groups], [bias_w] * NSA_KV_HEADS))

    o_ref[0] = (gate(0) * ocmp_ref[...] + gate(1) * o_slc + gate(2) * o_win).astype(o_ref.dtype)


def _nsa(qa, kvc, ot, kvs, ovs, kvw, ovw, e, misc, tq, tk):
    b, s, _ = qa.shape
    n_cmp = kvc.shape[2]
    hw = NSA_HEADS * HEAD_DIM
    assert WINDOW % tq == 0 and WINDOW + tq <= s
    gx = np.zeros((LANES, 3 * hw), np.float32)
    for branch in range(3):
        for h in range(NSA_HEADS):
            c0 = branch * hw + HEAD_DIM * h
            gx[_MISC_GA + branch * NSA_HEADS + h, c0:c0 + HEAD_DIM] = 1.0
    gx = jnp.asarray(gx, BF16)
    seq = lambda: pl.BlockSpec((1, s, 2 * LANES), lambda i, j: (i, 0, 0))
    return pl.pallas_call(
        _nsa_kernel,
        grid=(b, s // tq),
        in_specs=[pl.BlockSpec((1, tq, hw), lambda i, j: (i, j, 0)),
                  pl.BlockSpec((1, NSA_KV_HEADS, n_cmp, LANES), lambda i, j: (i, 0, 0, 0)),
                  pl.BlockSpec(ot.shape, lambda i, j: (0, 0)),
                  seq(), seq(), seq(), seq(),
                  pl.BlockSpec(e.shape, lambda i, j: (0, 0, 0)),
                  pl.BlockSpec((1, tq, LANES), lambda i, j: (i, j, 0)),
                  pl.BlockSpec(gx.shape, lambda i, j: (0, 0))],
        out_specs=pl.BlockSpec((1, tq, hw), lambda i, j: (i, j, 0)),
        out_shape=jax.ShapeDtypeStruct((b, s, hw), BF16),
        scratch_shapes=[pltpu.VMEM((NSA_HEADS * tq, LANES), BF16),
                        pltpu.VMEM((NSA_KV_HEADS, tq, LANES), BF16),
                        pltpu.VMEM((tq, hw), F32)] + _flash_scratch(NSA_HEADS, tq, tk),
        compiler_params=pltpu.CompilerParams(dimension_semantics=("arbitrary", "arbitrary"),
                                             vmem_limit_bytes=VMEM_LIMIT),
    )(qa, kvc, ot, kvs, ovs, kvw, ovw, e, misc, gx)


def _dsa_kernel(qb_ref, kvb_ref, ovb_ref, iq_ref, ik_ref, misc_ref, l2_ref, o_ref,
                score_ref, q_ref, *flash, topk, n_bisect):
    tq = qb_ref.shape[1]
    ts = score_ref.shape[1]
    ta = 2 * ts
    fold = ts // SUBLANES
    qi = pl.program_id(1)
    q0 = qi * tq
    n_pair = (q0 + tq + ta - 1) // ta
    key = lax.broadcasted_iota(jnp.int32, (ts, tq), 0)
    qry = lax.broadcasted_iota(jnp.int32, (ts, tq), 1)

    def fold_min(x):
        return jnp.min(x.reshape(fold, SUBLANES, tq), axis=0)

    def fold_max(x):
        return jnp.max(x.reshape(fold, SUBLANES, tq), axis=0)

    def fold_sum(x):
        return jnp.sum(x.reshape(fold, SUBLANES, tq), axis=0)

    q_ref[0:IDX_HEADS * tq] = _stack_heads(iq_ref[0].astype(F32), IDX_HEADS)
    misc_t = misc_ref[0].T
    wts = [misc_t[_MISC_IW + h:_MISC_IW + h + 1, :] * (IDX_HEADS ** -0.5) for h in range(IDX_HEADS)]

    def logits(c):
        k0 = pl.multiple_of(c * ta, ta)
        return _dot_nt(ik_ref[0, pl.ds(k0, ta), :], q_ref[0:IDX_HEADS * tq])

    def score_body(c, carry):
        mn, mx = carry
        k0 = c * ta
        lg = logits(c)
        for j in range(2):
            rows = slice(j * ts, (j + 1) * ts)
            sc = wts[0] * jnp.maximum(lg[rows, 0:tq], 0.0)
            for h in range(1, IDX_HEADS):
                sc = sc + wts[h] * jnp.maximum(lg[rows, h * tq:(h + 1) * tq], 0.0)
            causal = (k0 + j * ts + key) <= (q0 + qry)
            score_ref[2 * c + j] = jnp.where(causal, sc, NEG_INF)
            mn = jnp.minimum(mn, fold_min(jnp.where(causal, sc, _FMAX)))
            mx = jnp.maximum(mx, fold_max(jnp.where(causal, sc, -_FMAX)))
        return mn, mx

    mn, mx = lax.fori_loop(0, n_pair, score_body, (jnp.full((SUBLANES, tq), _FMAX, F32),
                                                   jnp.full((SUBLANES, tq), -_FMAX, F32)))

    @pl.when(n_pair % 2 == 1)
    def _():
        score_ref[2 * n_pair] = jnp.full((ts, tq), NEG_INF, F32)
        score_ref[2 * n_pair + 1] = jnp.full((ts, tq), NEG_INF, F32)
    lo = jnp.min(mn, axis=0, keepdims=True)
    hi = jnp.max(mx, axis=0, keepdims=True)
    kf = jnp.minimum(q0 + lax.broadcasted_iota(jnp.int32, (1, tq), 1) + 1, topk).astype(F32)

    def midpoint(lo, hi):
        mid = lo + (hi - lo) * 0.5
        return jnp.where(mid >= hi, lo, mid)

    def count_gt(thr):
        def body(c, cnt):
            for j in range(4):
                cnt = cnt + fold_sum(jnp.where(score_ref[4 * c + j] > thr, 1.0, 0.0))
            return cnt
        cnt = lax.fori_loop(0, (n_pair + 1) // 2, body, jnp.zeros((SUBLANES, tq), F32))
        return jnp.sum(cnt, axis=0, keepdims=True)

    def bisect(_, carry):
        lo, hi = carry
        mid = midpoint(lo, hi)
        up = count_gt(mid) >= kf
        return jnp.where(up, mid, lo), jnp.where(up, hi, mid)

    lo, hi = lax.fori_loop(0, n_bisect, bisect, (lo, hi))

    def snap_cond(carry):
        lo, hi = carry
        return jnp.max(hi - lo) > 0.0

    def snap(carry):
        lo, hi = carry
        mid = midpoint(lo, hi)

        def body(c, carry):
            cnt, above, below = carry
            for j in range(2):
                sc = score_ref[2 * c + j]
                gt = sc > mid
                cnt = cnt + fold_sum(jnp.where(gt, 1.0, 0.0))
                above = jnp.minimum(above, fold_min(jnp.where(gt, sc, _FMAX)))
                below = jnp.maximum(below, fold_max(jnp.where(gt, -_FMAX, sc)))
            return cnt, above, below

        cnt, above, below = lax.fori_loop(
            0, n_pair, body, (jnp.zeros((SUBLANES, tq), F32), jnp.full((SUBLANES, tq), _FMAX, F32),
                              jnp.full((SUBLANES, tq), -_FMAX, F32)))
        up = jnp.sum(cnt, axis=0, keepdims=True) >= kf
        return (jnp.where(up, jnp.min(above, axis=0, keepdims=True), lo),
                jnp.where(up, hi, jnp.max(below, axis=0, keepdims=True)))

    v, _ = lax.while_loop(snap_cond, lambda carry: snap(snap(carry)), (lo, hi))
    need = kf - count_gt(v)

    q_ref[...] = _stack_heads(qb_ref[0].astype(F32), DSA_HEADS)

    def att_kv(c):
        k0 = pl.multiple_of(c * ta, ta)
        return [kvb_ref[0, pl.ds(k0, ta), :]], [ovb_ref[0, pl.ds(k0, ta), :]]

    def att_bias(c, run):
        halves = []
        for j in range(2):
            sc = score_ref[2 * c + j]
            eq = sc == v
            pt = _dot(l2_ref[...], jnp.where(eq, 1.0, 0.0).astype(BF16))
            take = (sc > v) | (eq & (run + pt[:ts] < need))
            halves.append(jnp.where(take, 0.0, NEG_INF).T)
            run = run + pt[ts:ts + 1]
        return [jnp.concatenate(halves, axis=1)], run

    accs = _flash_loop(n_pair, 1, q_ref, att_kv, att_bias, jnp.zeros((1, tq), F32), flash)
    o_ref[0] = _flash_finish(accs).astype(o_ref.dtype)


def _dsa(qb, kvb, ovb, iq, ik, misc, l2, tq, topk, n_bisect):
    b, s, _ = qb.shape
    hw = DSA_HEADS * HEAD_DIM
    ts = l2.shape[1]
    assert (s // ts) % 4 == 0 and tq % (2 * ts) == 0 or tq == ts
    return pl.pallas_call(
        functools.partial(_dsa_kernel, topk=topk, n_bisect=n_bisect),
        grid=(b, s // tq),
        in_specs=[pl.BlockSpec((1, tq, hw), lambda i, j: (i, j, 0)),
                  pl.BlockSpec((1, s, LANES), lambda i, j: (i, 0, 0)),
                  pl.BlockSpec((1, s, LANES), lambda i, j: (i, 0, 0)),
                  pl.BlockSpec((1, tq, IDX_HEADS * IDX_DIM), lambda i, j: (i, j, 0)),
                  pl.BlockSpec((1, s, LANES), lambda i, j: (i, 0, 0)),
                  pl.BlockSpec((1, tq, LANES), lambda i, j: (i, j, 0)),
                  pl.BlockSpec(l2.shape, lambda i, j: (0, 0))],
        out_specs=pl.BlockSpec((1, tq, hw), lambda i, j: (i, j, 0)),
        out_shape=jax.ShapeDtypeStruct((b, s, hw), BF16),
        scratch_shapes=[pltpu.VMEM((s // ts, ts, tq), F32),
                        pltpu.VMEM((DSA_HEADS * tq, LANES), BF16)] + _flash_scratch(DSA_HEADS, tq, 2 * ts),
        compiler_params=pltpu.CompilerParams(dimension_semantics=("arbitrary", "arbitrary"),
                                             vmem_limit_bytes=VMEM_LIMIT),
    )(qb, kvb, ovb, iq, ik, misc, l2)


def _post_kernel(on_ref, od_ref, gm_ref, x_ref, wbn_ref, wbd_ref, wo_ref, gffn_ref, wg_ref, wu_ref, wd_ref,
                 gfin_ref, out_ref):
    ya = _dot(on_ref[...], wbn_ref[...])
    yb = _dot(od_ref[...], wbd_ref[...])
    gm = gm_ref[...]
    merged = _sigmoid(gm[:, :D_MODEL]) * ya + _sigmoid(gm[:, D_MODEL:]) * yb
    h = x_ref[...] + _dot(merged.astype(BF16), wo_ref[...])
    ms = jnp.mean(h * h, axis=-1, keepdims=True)
    hn = (h * lax.rsqrt(ms + NORM_EPS) * gffn_ref[...]).astype(BF16)
    acc = jnp.zeros_like(h)
    fc = 2 * LANES
    for c in range(wg_ref.shape[1] // fc):
        cols = slice(c * fc, (c + 1) * fc)
        gt = _dot(hn, wg_ref[:, cols])
        up = _dot(hn, wu_ref[:, cols])
        act = (gt * _sigmoid(gt) * up).astype(BF16)
        acc = acc + _dot(act, wd_ref[cols, :])
    h2 = h + acc
    ms2 = jnp.mean(h2 * h2, axis=-1, keepdims=True)
    out_ref[...] = h2 * lax.rsqrt(ms2 + NORM_EPS) * gfin_ref[...]


def _post(o_nsa, o_dsa, gm, x2, wbn, wbd, wo, gffn, wg, wu, wd, gfin, tm):
    n = x2.shape[0]
    row = lambda width: pl.BlockSpec((tm, width), lambda i: (i, 0))
    full = lambda a: pl.BlockSpec(a.shape, lambda i: (0,) * a.ndim, pipeline_mode=pl.Buffered(1))
    return pl.pallas_call(
        _post_kernel,
        grid=(n // tm,),
        in_specs=[row(o_nsa.shape[1]), row(o_dsa.shape[1]), row(2 * D_MODEL), row(D_MODEL),
                  full(wbn), full(wbd), full(wo), full(gffn), full(wg), full(wu), full(wd), full(gfin)],
        out_specs=row(D_MODEL),
        out_shape=jax.ShapeDtypeStruct((n, D_MODEL), F32),
        compiler_params=pltpu.CompilerParams(dimension_semantics=("arbitrary",), vmem_limit_bytes=VMEM_LIMIT),
    )(o_nsa, o_dsa, gm, x2, wbn, wbd, wo, gffn, wg, wu, wd, gfin)


def _layer(h2, b, s, norm_mix, w_in, cmp_pos_k, cmp_w1_k, cmp_w2_k, cmp_pos_v, cmp_w1_v, cmp_w2_v,
           w_branch_nsa, w_branch_dsa, w_out, norm_ffn, w_gate, w_up, w_down, norm_out):
    n_chunk = s // CMP_STRIDE
    n_slc = s // SLC_BLOCK
    tabs = _rope_tables(s)
    qa, ck0, ck1, cv0, cv1, kvs, ovs, kvw, ovw, qb, kvb, ovb, iq, ik, misc, gm = _proj(
        h2, norm_mix[None, :], _relayout_w_in(w_in), tabs, s, _TM)
    seq3 = lambda a: a.reshape(b, s, -1)

    pos = jnp.stack([cmp_pos_k, cmp_pos_v]).reshape(2, 2, 1, CMP_STRIDE * HEAD_DIM)
    w1 = jnp.stack([cmp_w1_k, cmp_w1_v]).astype(BF16)
    zpad = jnp.zeros((CMP_HIDDEN, HEAD_DIM), F32)
    w2 = jnp.stack([jnp.concatenate([cmp_w2_k, zpad], axis=1),
                    jnp.concatenate([zpad, cmp_w2_v], axis=1)]).astype(BF16)
    kvcmp = _compress((ck0, ck1, cv0, cv1), b, pos, w1, w2)

    ci = np.arange(n_chunk)[None, :] * CMP_STRIDE
    sj = np.arange(n_slc)[:, None] * SLC_BLOCK
    ot = ((ci < sj + SLC_BLOCK) & (ci + CMP_BLOCK > sj) & (np.arange(n_chunk)[None, :] < n_chunk - 1))

    assert n_slc <= LANES
    e = (np.arange(s)[None, :] // SLC_BLOCK == np.arange(LANES)[:, None])
    e = jnp.asarray(e.reshape(LANES, s // _TK, _TK).transpose(1, 0, 2), BF16)
    o_nsa = _nsa(seq3(qa), kvcmp, jnp.asarray(ot, BF16), seq3(kvs), seq3(ovs), seq3(kvw), seq3(ovw), e,
                 seq3(misc), _TQ, _TK)

    idx = np.arange(LANES)
    l2 = jnp.asarray(np.concatenate([idx[:, None] > idx[None, :], np.ones((LANES, LANES), bool)], axis=0), BF16)
    o_dsa = _dsa(seq3(qb), seq3(kvb), seq3(ovb), seq3(iq), seq3(ik), seq3(misc), l2, _TQ,
                 min(DSA_TOPK_MAX, s // 4), _N_BISECT)

    wg, wu, wd = w_gate.astype(BF16), w_up.astype(BF16), w_down.astype(BF16)
    return _post(o_nsa.reshape(b * s, -1), o_dsa.reshape(b * s, -1), gm, h2,
                 w_branch_nsa.astype(BF16), w_branch_dsa.astype(BF16), w_out.astype(BF16),
                 norm_ffn[None, :], wg, wu, wd, norm_out[None, :], _TM)


def kernel(x, norm_mix, w_in, cmp_pos_k, cmp_w1_k, cmp_w2_k, cmp_pos_v, cmp_w1_v, cmp_w2_v, w_branch_nsa,
           w_branch_dsa, w_out, norm_ffn, w_gate, w_up, w_down, norm_final):
    b, s, d = x.shape
    depth = norm_mix.shape[0]
    assert depth == 1, "the fused epilogue applies the final norm right after the single layer"
    out = _layer(x.reshape(b * s, d), b, s, norm_mix[0], w_in[0], cmp_pos_k[0], cmp_w1_k[0], cmp_w2_k[0],
                 cmp_pos_v[0], cmp_w1_v[0], cmp_w2_v[0], w_branch_nsa[0], w_branch_dsa[0], w_out[0],
                 norm_ffn[0], w_gate[0], w_up[0], w_down[0], norm_final)
    return out.reshape(b, s, d)
```

```python
import functools
import math

import numpy as np
import jax
import jax.numpy as jnp
from jax import lax
from jax.experimental import pallas as pl
from jax.experimental.pallas import tpu as pltpu

F32 = jnp.float32
BF16 = jnp.bfloat16

D_MODEL = 1024
HEAD_DIM = 64
ROPE_THETA = 10000.0
NORM_EPS = 1e-6
NEG_INF = -1e30
BIG = 1e4
_FMAX = 3.0e38
NSA_HEADS = 8
NSA_KV_HEADS = 2
NSA_GROUP = NSA_HEADS // NSA_KV_HEADS
CMP_BLOCK = 32
CMP_STRIDE = 16
CMP_HIDDEN = 256
SLC_BLOCK = 64
SLC_TOPN = 8
SLC_LOCAL = 2
WINDOW = 512
DSA_HEADS = 8
IDX_HEADS = 4
IDX_DIM = 64
DSA_TOPK_MAX = 256
D_FF = -(-8 * D_MODEL // (3 * 256)) * 256

LANES = 128
SUBLANES = 8
VMEM_LIMIT = 56 * 1024 * 1024

_TM = 512
_TQ = 2 * LANES
_TK = 2 * LANES
_N_BISECT = 12

_O_QA, _O_KC, _O_VC, _O_KSL, _O_VSL, _O_KWN, _O_VWN = 0, 512, 640, 768, 896, 1024, 1152
_O_GA, _O_QB, _O_KB, _O_VB, _O_IQ, _O_IK, _O_IW, _O_GM = 1280, 1304, 1816, 1880, 1944, 2200, 2264, 2268
_D_IN = 4316
_N_QA, _N_KVC, _N_KVS, _N_KVW, _N_QB, _N_KVB, _N_IK, _N_IQ, _N_MISC, _N_GM = (
    0, 512, 768, 1024, 1280, 1792, 1920, 2048, 2304, 2432)
_D_IN_PAD = 4480
_MISC_GA, _MISC_IW = 0, 24


def _dot(a, b):
    return jnp.dot(a, b, preferred_element_type=F32)


def _dot_nt(a, b):
    return lax.dot_general(a, b, (((1,), (1,)), ((), ())), preferred_element_type=F32)


def _sigmoid(x):
    return 1.0 / (1.0 + jnp.exp(-x))


def _stack_heads(x, n_heads):
    tq = x.shape[0]
    low = lax.broadcasted_iota(jnp.int32, (tq, LANES), 1) < HEAD_DIM
    parts = []
    for p in range(n_heads // 2):
        slab = x[:, LANES * p:LANES * (p + 1)]
        parts.append(jnp.where(low, slab, 0.0))
        parts.append(jnp.where(low, pltpu.roll(slab, HEAD_DIM, 1), 0.0))
    return jnp.concatenate(parts, axis=0).astype(BF16)


def _unstack_heads(acc, n_heads):
    tq = acc.shape[0] // n_heads
    low = lax.broadcasted_iota(jnp.int32, (tq, LANES), 1) < HEAD_DIM
    outs = []
    for p in range(n_heads // 2):
        a0 = acc[(2 * p) * tq:(2 * p + 1) * tq]
        a1 = acc[(2 * p + 1) * tq:(2 * p + 2) * tq]
        outs.append(jnp.where(low, pltpu.roll(a0, HEAD_DIM, 1), a1))
    return jnp.concatenate(outs, axis=1)


def _flash_loop_t(n_chunks, n_heads, q_ref, kv_at, bias_at, carry, scratch):
    s_ref, mx_ref, p_ref, m_ref, acc_ref = scratch
    tq = q_ref.shape[0] // n_heads
    cols = [slice(h * tq, (h + 1) * tq) for h in range(n_heads)]

    def groups(n_groups):
        per_group = n_heads // n_groups
        return per_group, [slice(j * per_group * tq, (j + 1) * per_group * tq) for j in range(n_groups)]

    def scores(c, carry):
        kvs, _ = kv_at(c)
        biases, carry = bias_at(c, carry)
        per_group, grp = groups(len(kvs))
        out = []
        for j, kv in enumerate(kvs):
            s_all = _dot_nt(kv, q_ref[grp[j]])
            for r in range(per_group):
                s = s_all[:, r * tq:(r + 1) * tq] + biases[j]
                out.append((s, jnp.max(s, axis=0, keepdims=True)))
        return out, carry

    def put_scores(s):
        for h in range(n_heads):
            s_ref[:, cols[h]] = s[h][0]
            mx_ref[:, cols[h]] = s[h][1]

    def values(c, p, acc):
        _, ovts = kv_at(c)
        per_group, _ = groups(len(ovts))
        out = []
        for j, ovt in enumerate(ovts):
            heads = range(j * per_group, (j + 1) * per_group)
            pv = _dot(ovt, jnp.concatenate([p[h] for h in heads], axis=1))
            out.extend(acc[h] + pv[:, r * tq:(r + 1) * tq] for r, h in enumerate(heads))
        return out

    m_ref[...] = jnp.full(m_ref.shape, NEG_INF, F32)
    acc_ref[...] = jnp.zeros(acc_ref.shape, F32)
    p_ref[...] = jnp.zeros(p_ref.shape, BF16)
    s0, carry = scores(0, carry)
    put_scores(s0)

    def body(c, carry):
        acc_new = values(jnp.maximum(c - 1, 0), [p_ref[:, cols[h]] for h in range(n_heads)],
                         [acc_ref[:, cols[h]] for h in range(n_heads)])
        m_new, p_new = [], []
        for h in range(n_heads):
            d = jnp.minimum(m_ref[:, cols[h]] - mx_ref[:, cols[h]], 0.0)
            m_h = m_ref[:, cols[h]] - d
            acc_new[h] = acc_new[h] * jnp.exp2(d)
            p_new.append(jnp.exp2(s_ref[:, cols[h]] - m_h).astype(BF16))
            m_new.append(m_h)
        s_next, carry = scores(jnp.minimum(c + 1, n_chunks - 1), carry)
        for h in range(n_heads):
            acc_ref[:, cols[h]] = acc_new[h]
            m_ref[:, cols[h]] = m_new[h]
            p_ref[:, cols[h]] = p_new[h]
        put_scores(s_next)
        return carry

    lax.fori_loop(0, n_chunks, body, carry)
    return values(n_chunks - 1, [p_ref[:, cols[h]] for h in range(n_heads)],
                  [acc_ref[:, cols[h]] for h in range(n_heads)])


def _softmax_once_t(q_ref, n_heads, kvs, ovts, biases):
    per_group = n_heads // len(kvs)
    tq = q_ref.shape[0] // n_heads
    out = []
    for j, (kv, ovt, bias) in enumerate(zip(kvs, ovts, biases)):
        s_all = _dot_nt(kv, q_ref[j * per_group * tq:(j + 1) * per_group * tq])
        p = []
        for r in range(per_group):
            s = s_all[:, r * tq:(r + 1) * tq] + bias
            p.append(jnp.exp2(s - jnp.max(s, axis=0, keepdims=True)).astype(BF16))
        pv = _dot(ovt, jnp.concatenate(p, axis=1))
        out.extend(pv[:, r * tq:(r + 1) * tq] for r in range(per_group))
    return out


def _flash_scratch_t(n_heads, tq, tk):
    return [pltpu.VMEM((tk, n_heads * tq), F32), pltpu.VMEM((1, n_heads * tq), F32),
            pltpu.VMEM((tk, n_heads * tq), BF16), pltpu.VMEM((1, n_heads * tq), F32),
            pltpu.VMEM((LANES, n_heads * tq), F32)]


def _flash_finish_t(accs):
    tq = accs[0].shape[1]
    outs = []
    for p in range(len(accs) // 2):
        a0, a1 = accs[2 * p], accs[2 * p + 1]
        num = jnp.concatenate([a0[HEAD_DIM:], a1[HEAD_DIM:]], axis=0)
        den = jnp.concatenate([jnp.broadcast_to(a0[0:1], (HEAD_DIM, tq)),
                               jnp.broadcast_to(a1[0:1], (HEAD_DIM, tq))], axis=0)
        outs.append((num / den).T)
    return jnp.concatenate(outs, axis=1)


_ROPE_Q, _ROPE_IQ, _ROPE_K, _ROPE_KV = 0, 1, 2, 3


def _proj_kernel(x_ref, g_ref, w_ref, tab_ref, qa_ref, ck0_ref, ck1_ref, cv0_ref, cv1_ref, kvs_ref, ovs_ref,
                 kvw_ref, ovw_ref, qb_ref, kvb_ref, ovb_ref, iq_ref, ik_ref, misc_ref, gm_ref, kc_ref, vc_ref):
    x = x_ref[...]
    tm = x.shape[0]
    ms = jnp.mean(x * x, axis=-1, keepdims=True)
    xn = (x * lax.rsqrt(ms + NORM_EPS) * g_ref[...]).astype(BF16)
    lane = lax.broadcasted_iota(jnp.int32, (tm, LANES), 1)
    low32 = (lane & (HEAD_DIM // 2)) == 0
    low64 = lane < HEAD_DIM

    def rope(z, kind):
        cos = tab_ref[2 * kind]
        sin = tab_ref[2 * kind + 1]
        rot = jnp.where(low32, pltpu.roll(z, LANES - HEAD_DIM // 2, 1), pltpu.roll(z, HEAD_DIM // 2, 1))
        return z * cos + rot * sin

    def emit_slabs(col, slabs):
        z = _dot(xn, w_ref[:, col:col + LANES * len(slabs)])
        for j, (out_ref, out_col, kind, ones_ref) in enumerate(slabs):
            zj = z[:, LANES * j:LANES * (j + 1)]
            if kind is not None:
                zj = rope(zj, kind)
            out_ref[:, out_col:out_col + LANES] = zj.astype(out_ref.dtype)
            if ones_ref is not None:
                ones_ref[:, out_col:out_col + LANES] = jnp.where(low64, 1.0, zj).astype(ones_ref.dtype)

    def emit(col, width, out_ref, out_col, kinds, ones_ref=None):
        emit_slabs(col, [(out_ref, out_col + LANES * j, kinds[j], ones_ref) for j in range(width // LANES)])

    emit(_N_QA, 512, qa_ref, 0, [_ROPE_Q] * 4)
    emit_slabs(_N_KVC, [(kc_ref, 0, _ROPE_K, None), (vc_ref, 0, None, None)])
    n_ck = tm // CMP_STRIDE
    low64c = lax.broadcasted_iota(jnp.int32, (n_ck, LANES), 1) < HEAD_DIM
    for src_ref, c_h0, c_h1 in ((kc_ref, ck0_ref, ck1_ref), (vc_ref, cv0_ref, cv1_ref)):
        for i in range(CMP_STRIDE // 2):
            ta = src_ref[pl.ds(2 * i, n_ck, stride=CMP_STRIDE), :]
            tb = src_ref[pl.ds(2 * i + 1, n_ck, stride=CMP_STRIDE), :]
            cols = slice(LANES * i, LANES * (i + 1))
            c_h0[:, cols] = jnp.where(low64c, ta, pltpu.roll(tb, HEAD_DIM, 1))
            c_h1[:, cols] = jnp.where(low64c, pltpu.roll(ta, HEAD_DIM, 1), tb)
    emit(_N_KVS, 256, kvs_ref, 0, [_ROPE_KV] * 2, ovs_ref)
    emit(_N_KVW, 256, kvw_ref, 0, [_ROPE_KV] * 2, ovw_ref)
    emit(_N_QB, 512, qb_ref, 0, [_ROPE_Q] * 4)
    emit_slabs(_N_KVB, [(kvb_ref, 0, _ROPE_KV, ovb_ref), (ik_ref, 0, _ROPE_K, None)])
    emit(_N_IQ, 256, iq_ref, 0, [_ROPE_IQ] * 2)
    emit(_N_MISC, 128, misc_ref, 0, [None])
    for c in range(4):
        emit(_N_GM + 512 * c, 512, gm_ref, 512 * c, [None] * 4)


def _relayout_w_in(w):
    z = lambda n: jnp.zeros((w.shape[0], n), w.dtype)
    s = lambda a, n: w[:, a:a + n]
    cols = [
        s(_O_QA, 512), s(_O_KC, 128), s(_O_VC, 128),
        s(_O_KSL, 64), s(_O_VSL, 64), s(_O_KSL + 64, 64), s(_O_VSL + 64, 64),
        s(_O_KWN, 64), s(_O_VWN, 64), s(_O_KWN + 64, 64), s(_O_VWN + 64, 64),
        s(_O_QB, 512), s(_O_KB, 64), s(_O_VB, 64),
        s(_O_IK, 64), z(64), s(_O_IQ, 256),
        s(_O_GA, 24), s(_O_IW, 4), z(100),
        s(_O_GM, 2048),
    ]
    out = jnp.concatenate(cols, axis=1)
    assert out.shape[1] == _D_IN_PAD
    return out.astype(BF16)


def _rope_tables(seq):
    half = HEAD_DIM // 2
    inv_freq = ROPE_THETA ** (-jnp.arange(half, dtype=F32) / half)
    ang = jnp.arange(seq, dtype=F32)[:, None] * inv_freq[None, :]
    cos, sin = jnp.cos(ang), jnp.sin(ang)
    cos64 = jnp.concatenate([cos, cos], axis=1)
    sin64 = jnp.concatenate([-sin, sin], axis=1)
    cos_k = jnp.concatenate([cos64, cos64], axis=1)
    sin_k = jnp.concatenate([sin64, sin64], axis=1)
    scale = HEAD_DIM ** -0.5
    q_scale = scale * math.log2(math.e)
    cos_kv = jnp.concatenate([cos64, jnp.ones_like(cos64)], axis=1)
    sin_kv = jnp.concatenate([sin64, jnp.zeros_like(sin64)], axis=1)
    return jnp.stack([cos_k * q_scale, sin_k * q_scale, cos_k * scale, sin_k * scale,
                      cos_k, sin_k, cos_kv, sin_kv], axis=0)


def _proj(x2, gain, w_pad, tabs, seq, tm):
    n = x2.shape[0]
    nblk_seq = seq // tm
    row = lambda width: pl.BlockSpec((tm, width), lambda i: (i, 0))
    full = lambda shape: pl.BlockSpec(shape, lambda i: (0,) * len(shape))
    outs = [(1, 512, BF16)] + [(CMP_STRIDE, CMP_STRIDE * HEAD_DIM, F32)] * 4 + [
        (1, 256, BF16), (1, 256, BF16), (1, 256, BF16), (1, 256, BF16), (1, 512, BF16),
        (1, 128, BF16), (1, 128, BF16), (1, 256, BF16), (1, 128, BF16), (1, 128, F32), (1, 2048, F32)]
    return pl.pallas_call(
        _proj_kernel,
        grid=(n // tm,),
        in_specs=[row(D_MODEL), full((1, D_MODEL)), full((D_MODEL, _D_IN_PAD)),
                  pl.BlockSpec((tabs.shape[0], tm, LANES), lambda i: (0, i % nblk_seq, 0))],
        out_specs=[pl.BlockSpec((tm // div, wd), lambda i: (i, 0)) for div, wd, _ in outs],
        out_shape=[jax.ShapeDtypeStruct((n // div, wd), dt) for div, wd, dt in outs],
        scratch_shapes=[pltpu.VMEM((tm, LANES), F32), pltpu.VMEM((tm, LANES), F32)],
        compiler_params=pltpu.CompilerParams(dimension_semantics=("arbitrary",), vmem_limit_bytes=VMEM_LIMIT),
    )(x2, gain, w_pad, tabs)


def _compress_kernel(ck0_ref, ck1_ref, cv0_ref, cv1_ref, pos_ref, w1_ref, w2_ref, out_ref):
    c_refs = ((ck0_ref, ck1_ref), (cv0_ref, cv1_ref))
    n_chunk = ck0_ref.shape[0]
    for h in range(NSA_KV_HEADS):
        acc = jnp.zeros((n_chunk, LANES), F32)
        for kv in range(2):
            c = c_refs[kv][h][...]
            a_lo = (c + pos_ref[kv, 0]).astype(BF16)
            a_hi = (c + pos_ref[kv, 1]).astype(BF16)
            half = CMP_STRIDE * HEAD_DIM
            h_lo = _dot(a_lo, w1_ref[kv, :half, :])
            h_hi = _dot(a_hi, w1_ref[kv, half:, :])
            hid = h_lo + pltpu.roll(h_hi, n_chunk - 1, 0)
            act = jax.nn.gelu(hid, approximate=True).astype(BF16)
            acc = acc + _dot(act, w2_ref[kv])
        out_ref[0, h] = acc.astype(out_ref.dtype)


def _compress(cs, b, pos, w1, w2):
    n_chunk = cs[0].shape[0] // b
    return pl.pallas_call(
        _compress_kernel,
        grid=(b,),
        in_specs=[pl.BlockSpec((n_chunk, c.shape[1]), lambda i: (i, 0)) for c in cs] + [
                  pl.BlockSpec(pos.shape, lambda i: (0, 0, 0, 0)),
                  pl.BlockSpec(w1.shape, lambda i: (0, 0, 0)),
                  pl.BlockSpec(w2.shape, lambda i: (0, 0, 0))],
        out_specs=pl.BlockSpec((1, NSA_KV_HEADS, n_chunk, LANES), lambda i: (i, 0, 0, 0)),
        out_shape=jax.ShapeDtypeStruct((b, NSA_KV_HEADS, n_chunk, LANES), BF16),
        compiler_params=pltpu.CompilerParams(dimension_semantics=("arbitrary",), vmem_limit_bytes=VMEM_LIMIT),
    )(*cs, pos, w1, w2)


def _cmp_branch(q_ref, kvc_ref, ot_ref, ocmp_ref, sel_ref, q0, tq):
    n_cmp = kvc_ref.shape[2]
    n_slc = ot_ref.shape[0]
    t3 = q0 + lax.broadcasted_iota(jnp.int32, (NSA_GROUP, tq, n_cmp), 1)
    n3 = lax.broadcasted_iota(jnp.int32, (NSA_GROUP, tq, n_cmp), 2)
    mask = (n3 * CMP_STRIDE + (CMP_BLOCK - 1)) <= t3
    j = lax.broadcasted_iota(jnp.int32, (n_slc, tq), 0)
    cur = (q0 + lax.broadcasted_iota(jnp.int32, (n_slc, tq), 1)) // SLC_BLOCK
    forced = (j == 0) | ((cur - j >= 0) & (cur - j < SLC_LOCAL))
    adm = j <= cur
    for g in range(NSA_KV_HEADS):
        gw = NSA_GROUP * HEAD_DIM
        q4 = q_ref[g * NSA_GROUP * tq:(g + 1) * NSA_GROUP * tq]
        kv = kvc_ref[0, g]
        s = _dot_nt(q4, kv).reshape(NSA_GROUP, tq, n_cmp)
        s = jnp.where(mask, s, NEG_INF)
        m = jnp.max(s, axis=2, keepdims=True)
        e = jnp.where(mask, jnp.exp2(s - m), 0.0)
        l = jnp.sum(e, axis=2, keepdims=True)
        p = e * jnp.where(l > 0.0, 1.0 / l, 0.0)
        o = _dot(p.reshape(NSA_GROUP * tq, n_cmp).astype(BF16), kv)
        ocmp_ref[:, gw * g:gw * (g + 1)] = _unstack_heads(o, NSA_GROUP)
        psum = p[0] + p[1] + p[2] + p[3]
        hi = psum.astype(BF16)
        lo = (psum - hi.astype(F32)).astype(BF16)
        p_slc = _dot_nt(ot_ref[...], hi) + _dot_nt(ot_ref[...], lo)
        blk = jnp.where(forced, BIG, p_slc)
        blk = jnp.where(adm, blk, NEG_INF)
        rank = jnp.zeros((n_slc, tq), F32)
        for i in range(n_slc):
            bi = blk[i:i + 1, :]
            beats = (bi > blk) | ((bi == blk) & (j > i))
            rank = rank + jnp.where(beats, 1.0, 0.0)
        selt = jnp.where((rank < float(SLC_TOPN)) & adm, 1.0, 0.0)
        sel_ref[g] = jnp.concatenate([selt, jnp.zeros((LANES - n_slc, tq), F32)], axis=0).astype(BF16)


def _nsa_kernel(qa_ref, kvc_ref, ot_ref, kvs_ref, ovst_ref, kvw_ref, ovwt_ref, e_ref, misc_ref, gx_ref, o_ref,
                q_ref, sel_ref, ocmp_ref, *flash):
    tk = flash[0].shape[0]
    tq = qa_ref.shape[1]
    qi = pl.program_id(1)
    q0 = qi * tq
    span = WINDOW + tq
    gsig = _sigmoid(misc_ref[0])
    key_s = lax.broadcasted_iota(jnp.int32, (tk, tq), 0)
    t_s = q0 + lax.broadcasted_iota(jnp.int32, (tk, tq), 1)
    hw = NSA_HEADS * HEAD_DIM
    groups = range(NSA_KV_HEADS)
    cols = [slice(LANES * g, LANES * (g + 1)) for g in groups]
    g_hi = gsig.astype(BF16)
    g_lo = (gsig - g_hi.astype(F32)).astype(BF16)
    gates = _dot(g_hi, gx_ref[...]) + _dot(g_lo, gx_ref[...])

    def gate(branch):
        return gates[:, hw * branch:hw * (branch + 1)]

    q_ref[...] = _stack_heads(qa_ref[0].astype(F32), NSA_HEADS)
    _cmp_branch(q_ref, kvc_ref, ot_ref, ocmp_ref, sel_ref, q0, tq)

    def slc_kv(kt):
        k0 = pl.multiple_of(kt * tk, tk)
        return ([kvs_ref[0, pl.ds(k0, tk), cols[g]] for g in groups],
                [ovst_ref[0, kt, cols[g], :] for g in groups])

    def slc_bias(kt, carry):
        causal = (kt * tk + key_s) <= t_s
        return [jnp.where((_dot(e_ref[kt], sel_ref[g]) > 0.5) & causal, 0.0, NEG_INF) for g in groups], carry

    o_slc = _flash_finish_t(_flash_loop_t((q0 + tq + tk - 1) // tk, NSA_HEADS, q_ref, slc_kv, slc_bias, 0, flash))

    k_lo = pl.multiple_of(jnp.maximum(q0 + tq - span, 0), tk)
    c_lo = k_lo // tk
    dist = (q0 + lax.broadcasted_iota(jnp.int32, (span, tq), 1)) - (k_lo + lax.broadcasted_iota(
        jnp.int32, (span, tq), 0))
    bias_w = jnp.where((dist >= 0) & (dist < WINDOW), 0.0, NEG_INF)
    o_win = _flash_finish_t(_softmax_once_t(
        q_ref, NSA_HEADS, [kvw_ref[0, pl.ds(k_lo, span), cols[g]] for g in groups],
        [jnp.concatenate([ovwt_ref[0, c_lo + i, cols[g], :] for i in range(span // tk)], axis=1) for g in groups],
        [bias_w] * NSA_KV_HEADS))

    o_ref[0] = (gate(0) * ocmp_ref[...] + gate(1) * o_slc + gate(2) * o_win).astype(o_ref.dtype)


def _nsa(qa, kvc, ot, kvs, ovs, kvw, ovw, e, misc, tq, tk):
    b, s, _ = qa.shape
    n_cmp = kvc.shape[2]
    hw = NSA_HEADS * HEAD_DIM
    assert WINDOW % tq == 0 and WINDOW + tq <= s
    gx = np.zeros((LANES, 3 * hw), np.float32)
    for branch in range(3):
        for h in range(NSA_HEADS):
            c0 = branch * hw + HEAD_DIM * h
            gx[_MISC_GA + branch * NSA_HEADS + h, c0:c0 + HEAD_DIM] = 1.0
    gx = jnp.asarray(gx, BF16)
    assert tq == tk, "the window step addresses its value operand in whole key chunks"
    seq = lambda: pl.BlockSpec((1, s, 2 * LANES), lambda i, j: (i, 0, 0))
    seq_t = lambda: pl.BlockSpec((1, s // tk, 2 * LANES, tk), lambda i, j: (i, 0, 0, 0))
    ovs, ovw = (a.reshape(b, s // tk, tk, 2 * LANES).transpose(0, 1, 3, 2) for a in (ovs, ovw))
    return pl.pallas_call(
        _nsa_kernel,
        grid=(b, s // tq),
        in_specs=[pl.BlockSpec((1, tq, hw), lambda i, j: (i, j, 0)),
                  pl.BlockSpec((1, NSA_KV_HEADS, n_cmp, LANES), lambda i, j: (i, 0, 0, 0)),
                  pl.BlockSpec(ot.shape, lambda i, j: (0, 0)),
                  seq(), seq_t(), seq(), seq_t(),
                  pl.BlockSpec(e.shape, lambda i, j: (0, 0, 0)),
                  pl.BlockSpec((1, tq, LANES), lambda i, j: (i, j, 0)),
                  pl.BlockSpec(gx.shape, lambda i, j: (0, 0))],
        out_specs=pl.BlockSpec((1, tq, hw), lambda i, j: (i, j, 0)),
        out_shape=jax.ShapeDtypeStruct((b, s, hw), BF16),
        scratch_shapes=[pltpu.VMEM((NSA_HEADS * tq, LANES), BF16),
                        pltpu.VMEM((NSA_KV_HEADS, LANES, tq), BF16),
                        pltpu.VMEM((tq, hw), F32)] + _flash_scratch_t(NSA_HEADS, tq, tk),
        compiler_params=pltpu.CompilerParams(dimension_semantics=("arbitrary", "arbitrary"),
                                             vmem_limit_bytes=VMEM_LIMIT),
    )(qa, kvc, ot, kvs, ovs, kvw, ovw, e, misc, gx)


def _dsa_kernel(qb_ref, kvb_ref, ovt_ref, iq_ref, ik_ref, misc_ref, l2_ref, o_ref,
                score_ref, q_ref, *flash, topk, n_bisect):
    tq = qb_ref.shape[1]
    ts = score_ref.shape[1]
    ta = 2 * ts
    fold = ts // SUBLANES
    qi = pl.program_id(1)
    q0 = qi * tq
    n_pair = (q0 + tq + ta - 1) // ta
    key = lax.broadcasted_iota(jnp.int32, (ts, tq), 0)
    qry = lax.broadcasted_iota(jnp.int32, (ts, tq), 1)

    def fold_min(x):
        return jnp.min(x.reshape(fold, SUBLANES, tq), axis=0)

    def fold_max(x):
        return jnp.max(x.reshape(fold, SUBLANES, tq), axis=0)

    def fold_sum(x):
        return jnp.sum(x.reshape(fold, SUBLANES, tq), axis=0)

    q_ref[0:IDX_HEADS * tq] = _stack_heads(iq_ref[0].astype(F32), IDX_HEADS)
    misc_t = misc_ref[0].T
    wts = [misc_t[_MISC_IW + h:_MISC_IW + h + 1, :] * (IDX_HEADS ** -0.5) for h in range(IDX_HEADS)]

    def logits(c):
        k0 = pl.multiple_of(c * ta, ta)
        return _dot_nt(ik_ref[0, pl.ds(k0, ta), :], q_ref[0:IDX_HEADS * tq])

    def score_body(c, carry):
        mn, mx = carry
        k0 = c * ta
        lg = logits(c)
        for j in range(2):
            rows = slice(j * ts, (j + 1) * ts)
            sc = wts[0] * jnp.maximum(lg[rows, 0:tq], 0.0)
            for h in range(1, IDX_HEADS):
                sc = sc + wts[h] * jnp.maximum(lg[rows, h * tq:(h + 1) * tq], 0.0)
            causal = (k0 + j * ts + key) <= (q0 + qry)
            score_ref[2 * c + j] = jnp.where(causal, sc, NEG_INF)
            mn = jnp.minimum(mn, fold_min(jnp.where(causal, sc, _FMAX)))
            mx = jnp.maximum(mx, fold_max(jnp.where(causal, sc, -_FMAX)))
        return mn, mx

    mn, mx = lax.fori_loop(0, n_pair, score_body, (jnp.full((SUBLANES, tq), _FMAX, F32),
                                                   jnp.full((SUBLANES, tq), -_FMAX, F32)))

    @pl.when(n_pair % 2 == 1)
    def _():
        score_ref[2 * n_pair] = jnp.full((ts, tq), NEG_INF, F32)
        score_ref[2 * n_pair + 1] = jnp.full((ts, tq), NEG_INF, F32)
    lo = jnp.min(mn, axis=0, keepdims=True)
    hi = jnp.max(mx, axis=0, keepdims=True)
    kf = jnp.minimum(q0 + lax.broadcasted_iota(jnp.int32, (1, tq), 1) + 1, topk).astype(F32)

    def midpoint(lo, hi):
        mid = lo + (hi - lo) * 0.5
        return jnp.where(mid >= hi, lo, mid)

    def count_gt(thr):
        def body(c, cnt):
            for j in range(4):
                cnt = cnt + fold_sum(jnp.where(score_ref[4 * c + j] > thr, 1.0, 0.0))
            return cnt
        cnt = lax.fori_loop(0, (n_pair + 1) // 2, body, jnp.zeros((SUBLANES, tq), F32))
        return jnp.sum(cnt, axis=0, keepdims=True)

    def bisect(_, carry):
        lo, hi = carry
        mid = midpoint(lo, hi)
        up = count_gt(mid) >= kf
        return jnp.where(up, mid, lo), jnp.where(up, hi, mid)

    lo, hi = lax.fori_loop(0, n_bisect, bisect, (lo, hi))

    def snap_cond(carry):
        lo, hi = carry
        return jnp.max(hi - lo) > 0.0

    def snap(carry):
        lo, hi = carry
        mid = midpoint(lo, hi)

        def body(c, carry):
            cnt, above, below = carry
            for j in range(2):
                sc = score_ref[2 * c + j]
                gt = sc > mid
                cnt = cnt + fold_sum(jnp.where(gt, 1.0, 0.0))
                above = jnp.minimum(above, fold_min(jnp.where(gt, sc, _FMAX)))
                below = jnp.maximum(below, fold_max(jnp.where(gt, -_FMAX, sc)))
            return cnt, above, below

        cnt, above, below = lax.fori_loop(
            0, n_pair, body, (jnp.zeros((SUBLANES, tq), F32), jnp.full((SUBLANES, tq), _FMAX, F32),
                              jnp.full((SUBLANES, tq), -_FMAX, F32)))
        up = jnp.sum(cnt, axis=0, keepdims=True) >= kf
        return (jnp.where(up, jnp.min(above, axis=0, keepdims=True), lo),
                jnp.where(up, hi, jnp.max(below, axis=0, keepdims=True)))

    v, _ = lax.while_loop(snap_cond, lambda carry: snap(snap(carry)), (lo, hi))
    need = kf - count_gt(v)

    q_ref[...] = _stack_heads(qb_ref[0].astype(F32), DSA_HEADS)

    def att_kv(c):
        k0 = pl.multiple_of(c * ta, ta)
        return [kvb_ref[0, pl.ds(k0, ta), :]], [ovt_ref[0, c]]

    def att_bias(c, run):
        halves = []
        for j in range(2):
            sc = score_ref[2 * c + j]
            eq = sc == v
            pt = _dot(l2_ref[...], jnp.where(eq, 1.0, 0.0).astype(BF16))
            take = (sc > v) | (eq & (run + pt[:ts] < need))
            halves.append(jnp.where(take, 0.0, NEG_INF))
            run = run + pt[ts:ts + 1]
        return [jnp.concatenate(halves, axis=0)], run

    accs = _flash_loop_t(n_pair, DSA_HEADS, q_ref, att_kv, att_bias, jnp.zeros((1, tq), F32), flash)
    o_ref[0] = _flash_finish_t(accs).astype(o_ref.dtype)


def _dsa(qb, kvb, ovb, iq, ik, misc, l2, tq, topk, n_bisect):
    b, s, _ = qb.shape
    hw = DSA_HEADS * HEAD_DIM
    ts = l2.shape[1]
    ta = 2 * ts
    assert (s // ts) % 4 == 0 and tq % ta == 0
    ovt = ovb.reshape(b, s // ta, ta, LANES).transpose(0, 1, 3, 2)
    return pl.pallas_call(
        functools.partial(_dsa_kernel, topk=topk, n_bisect=n_bisect),
        grid=(b, s // tq),
        in_specs=[pl.BlockSpec((1, tq, hw), lambda i, j: (i, j, 0)),
                  pl.BlockSpec((1, s, LANES), lambda i, j: (i, 0, 0)),
                  pl.BlockSpec((1, s // ta, LANES, ta), lambda i, j: (i, 0, 0, 0)),
                  pl.BlockSpec((1, tq, IDX_HEADS * IDX_DIM), lambda i, j: (i, j, 0)),
                  pl.BlockSpec((1, s, LANES), lambda i, j: (i, 0, 0)),
                  pl.BlockSpec((1, tq, LANES), lambda i, j: (i, j, 0)),
                  pl.BlockSpec(l2.shape, lambda i, j: (0, 0))],
        out_specs=pl.BlockSpec((1, tq, hw), lambda i, j: (i, j, 0)),
        out_shape=jax.ShapeDtypeStruct((b, s, hw), BF16),
        scratch_shapes=[pltpu.VMEM((s // ts, ts, tq), F32),
                        pltpu.VMEM((DSA_HEADS * tq, LANES), BF16)] + _flash_scratch_t(DSA_HEADS, tq, ta),
        compiler_params=pltpu.CompilerParams(dimension_semantics=("arbitrary", "arbitrary"),
                                             vmem_limit_bytes=VMEM_LIMIT),
    )(qb, kvb, ovt, iq, ik, misc, l2)


def _post_kernel(on_ref, od_ref, gm_ref, x_ref, wbn_ref, wbd_ref, wo_ref, gffn_ref, wg_ref, wu_ref, wd_ref,
                 gfin_ref, out_ref):
    ya = _dot(on_ref[...], wbn_ref[...])
    yb = _dot(od_ref[...], wbd_ref[...])
    gm = gm_ref[...]
    merged = _sigmoid(gm[:, :D_MODEL]) * ya + _sigmoid(gm[:, D_MODEL:]) * yb
    h = x_ref[...] + _dot(merged.astype(BF16), wo_ref[...])
    ms = jnp.mean(h * h, axis=-1, keepdims=True)
    hn = (h * lax.rsqrt(ms + NORM_EPS) * gffn_ref[...]).astype(BF16)
    acc = jnp.zeros_like(h)
    fc = 2 * LANES
    for c in range(wg_ref.shape[1] // fc):
        cols = slice(c * fc, (c + 1) * fc)
        gt = _dot(hn, wg_ref[:, cols])
        up = _dot(hn, wu_ref[:, cols])
        act = (gt * _sigmoid(gt) * up).astype(BF16)
        acc = acc + _dot(act, wd_ref[cols, :])
    h2 = h + acc
    ms2 = jnp.mean(h2 * h2, axis=-1, keepdims=True)
    out_ref[...] = h2 * lax.rsqrt(ms2 + NORM_EPS) * gfin_ref[...]


def _post(o_nsa, o_dsa, gm, x2, wbn, wbd, wo, gffn, wg, wu, wd, gfin, tm):
    n = x2.shape[0]
    row = lambda width: pl.BlockSpec((tm, width), lambda i: (i, 0))
    full = lambda a: pl.BlockSpec(a.shape, lambda i: (0,) * a.ndim, pipeline_mode=pl.Buffered(1))
    return pl.pallas_call(
        _post_kernel,
        grid=(n // tm,),
        in_specs=[row(o_nsa.shape[1]), row(o_dsa.shape[1]), row(2 * D_MODEL), row(D_MODEL),
                  full(wbn), full(wbd), full(wo), full(gffn), full(wg), full(wu), full(wd), full(gfin)],
        out_specs=row(D_MODEL),
        out_shape=jax.ShapeDtypeStruct((n, D_MODEL), F32),
        compiler_params=pltpu.CompilerParams(dimension_semantics=("arbitrary",), vmem_limit_bytes=VMEM_LIMIT),
    )(o_nsa, o_dsa, gm, x2, wbn, wbd, wo, gffn, wg, wu, wd, gfin)


def _layer(h2, b, s, norm_mix, w_in, cmp_pos_k, cmp_w1_k, cmp_w2_k, cmp_pos_v, cmp_w1_v, cmp_w2_v,
           w_branch_nsa, w_branch_dsa, w_out, norm_ffn, w_gate, w_up, w_down, norm_out):
    n_chunk = s // CMP_STRIDE
    n_slc = s // SLC_BLOCK
    tabs = _rope_tables(s)
    qa, ck0, ck1, cv0, cv1, kvs, ovs, kvw, ovw, qb, kvb, ovb, iq, ik, misc, gm = _proj(
        h2, norm_mix[None, :], _relayout_w_in(w_in), tabs, s, _TM)
    seq3 = lambda a: a.reshape(b, s, -1)

    pos = jnp.stack([cmp_pos_k, cmp_pos_v]).reshape(2, 2, 1, CMP_STRIDE * HEAD_DIM)
    w1 = jnp.stack([cmp_w1_k, cmp_w1_v]).astype(BF16)
    zpad = jnp.zeros((CMP_HIDDEN, HEAD_DIM), F32)
    w2 = jnp.stack([jnp.concatenate([cmp_w2_k, zpad], axis=1),
                    jnp.concatenate([zpad, cmp_w2_v], axis=1)]).astype(BF16)
    kvcmp = _compress((ck0, ck1, cv0, cv1), b, pos, w1, w2)

    ci = np.arange(n_chunk)[None, :] * CMP_STRIDE
    sj = np.arange(n_slc)[:, None] * SLC_BLOCK
    ot = ((ci < sj + SLC_BLOCK) & (ci + CMP_BLOCK > sj) & (np.arange(n_chunk)[None, :] < n_chunk - 1))

    assert n_slc <= LANES
    e = (np.arange(s)[None, :] // SLC_BLOCK == np.arange(LANES)[:, None])
    e = jnp.asarray(e.reshape(LANES, s // _TK, _TK).transpose(1, 2, 0), BF16)
    o_nsa = _nsa(seq3(qa), kvcmp, jnp.asarray(ot, BF16), seq3(kvs), seq3(ovs), seq3(kvw), seq3(ovw), e,
                 seq3(misc), _TQ, _TK)

    idx = np.arange(LANES)
    l2 = jnp.asarray(np.concatenate([idx[:, None] > idx[None, :], np.ones((LANES, LANES), bool)], axis=0), BF16)
    o_dsa = _dsa(seq3(qb), seq3(kvb), seq3(ovb), seq3(iq), seq3(ik), seq3(misc), l2, _TQ,
                 min(DSA_TOPK_MAX, s // 4), _N_BISECT)

    wg, wu, wd = w_gate.astype(BF16), w_up.astype(BF16), w_down.astype(BF16)
    return _post(o_nsa.reshape(b * s, -1), o_dsa.reshape(b * s, -1), gm, h2,
                 w_branch_nsa.astype(BF16), w_branch_dsa.astype(BF16), w_out.astype(BF16),
                 norm_ffn[None, :], wg, wu, wd, norm_out[None, :], _TM)


def kernel(x, norm_mix, w_in, cmp_pos_k, cmp_w1_k, cmp_w2_k, cmp_pos_v, cmp_w1_v, cmp_w2_v, w_branch_nsa,
           w_branch_dsa, w_out, norm_ffn, w_gate, w_up, w_down, norm_final):
    b, s, d = x.shape
    depth = norm_mix.shape[0]
    assert depth == 1, "the fused epilogue applies the final norm right after the single layer"
    out = _layer(x.reshape(b * s, d), b, s, norm_mix[0], w_in[0], cmp_pos_k[0], cmp_w1_k[0], cmp_w2_k[0],
                 cmp_pos_v[0], cmp_w1_v[0], cmp_w2_v[0], w_branch_nsa[0], w_branch_dsa[0], w_out[0],
                 norm_ffn[0], w_gate[0], w_up[0], w_down[0], norm_final)
    return out.reshape(b, s, d)
```

```python
import functools
import math

import numpy as np
import jax
import jax.numpy as jnp
from jax import lax
from jax.experimental import pallas as pl
from jax.experimental.pallas import tpu as pltpu

F32 = jnp.float32
BF16 = jnp.bfloat16

D_MODEL = 1024
HEAD_DIM = 64
ROPE_THETA = 10000.0
NORM_EPS = 1e-6
NEG_INF = -1e30
BIG = 1e4
_FMAX = 3.0e38
NSA_HEADS = 8
NSA_KV_HEADS = 2
NSA_GROUP = NSA_HEADS // NSA_KV_HEADS
CMP_BLOCK = 32
CMP_STRIDE = 16
CMP_HIDDEN = 256
SLC_BLOCK = 64
SLC_TOPN = 8
SLC_LOCAL = 2
WINDOW = 512
DSA_HEADS = 8
IDX_HEADS = 4
IDX_DIM = 64
DSA_TOPK_MAX = 256
D_FF = -(-8 * D_MODEL // (3 * 256)) * 256

LANES = 128
SUBLANES = 8
VMEM_LIMIT = 56 * 1024 * 1024

_TM = 512
_TQ = 2 * LANES
_TK = 2 * LANES
_N_BISECT = 12

_O_QA, _O_KC, _O_VC, _O_KSL, _O_VSL, _O_KWN, _O_VWN = 0, 512, 640, 768, 896, 1024, 1152
_O_GA, _O_QB, _O_KB, _O_VB, _O_IQ, _O_IK, _O_IW, _O_GM = 1280, 1304, 1816, 1880, 1944, 2200, 2264, 2268
_D_IN = 4316
_N_QA, _N_KVC, _N_KVS, _N_KVW, _N_QB, _N_KVB, _N_IK, _N_IQ, _N_MISC, _N_GM = (
    0, 512, 768, 1024, 1280, 1792, 1920, 2048, 2304, 2432)
_D_IN_PAD = 4480
_MISC_GA, _MISC_IW = 0, 24


def _dot(a, b):
    return jnp.dot(a, b, preferred_element_type=F32)


def _dot_nt(a, b):
    return lax.dot_general(a, b, (((1,), (1,)), ((), ())), preferred_element_type=F32)


def _sigmoid(x):
    return 1.0 / (1.0 + jnp.exp(-x))


def _stack_heads(x, n_heads):
    tq = x.shape[0]
    low = lax.broadcasted_iota(jnp.int32, (tq, LANES), 1) < HEAD_DIM
    parts = []
    for p in range(n_heads // 2):
        slab = x[:, LANES * p:LANES * (p + 1)]
        parts.append(jnp.where(low, slab, 0.0))
        parts.append(jnp.where(low, pltpu.roll(slab, HEAD_DIM, 1), 0.0))
    return jnp.concatenate(parts, axis=0).astype(BF16)


def _unstack_heads(acc, n_heads):
    tq = acc.shape[0] // n_heads
    low = lax.broadcasted_iota(jnp.int32, (tq, LANES), 1) < HEAD_DIM
    outs = []
    for p in range(n_heads // 2):
        a0 = acc[(2 * p) * tq:(2 * p + 1) * tq]
        a1 = acc[(2 * p + 1) * tq:(2 * p + 2) * tq]
        outs.append(jnp.where(low, pltpu.roll(a0, HEAD_DIM, 1), a1))
    return jnp.concatenate(outs, axis=1)


def _flash_loop_t(n_chunks, n_heads, q_ref, kv_at, bias_at, carry, scratch):
    s_ref, mx_ref, p_ref, m_ref, acc_ref = scratch
    tq = q_ref.shape[0] // n_heads
    cols = [slice(h * tq, (h + 1) * tq) for h in range(n_heads)]

    def groups(n_groups):
        per_group = n_heads // n_groups
        return per_group, [slice(j * per_group * tq, (j + 1) * per_group * tq) for j in range(n_groups)]

    def scores(c, carry):
        kvs, _ = kv_at(c)
        biases, carry = bias_at(c, carry)
        per_group, grp = groups(len(kvs))
        out = []
        for j, kv in enumerate(kvs):
            s_all = _dot_nt(kv, q_ref[grp[j]])
            for r in range(per_group):
                s = s_all[:, r * tq:(r + 1) * tq] + biases[j]
                out.append((s, jnp.max(s, axis=0, keepdims=True)))
        return out, carry

    def put_scores(s):
        for h in range(n_heads):
            s_ref[:, cols[h]] = s[h][0]
            mx_ref[:, cols[h]] = s[h][1]

    def values(c, p, acc):
        _, ovts = kv_at(c)
        per_group, _ = groups(len(ovts))
        out = []
        for j, ovt in enumerate(ovts):
            heads = range(j * per_group, (j + 1) * per_group)
            pv = _dot(ovt, jnp.concatenate([p[h] for h in heads], axis=1))
            out.extend(acc[h] + pv[:, r * tq:(r + 1) * tq] for r, h in enumerate(heads))
        return out

    m_ref[...] = jnp.full(m_ref.shape, NEG_INF, F32)
    acc_ref[...] = jnp.zeros(acc_ref.shape, F32)
    p_ref[...] = jnp.zeros(p_ref.shape, BF16)
    s0, carry = scores(0, carry)
    put_scores(s0)

    def body(c, carry):
        acc_new = values(jnp.maximum(c - 1, 0), [p_ref[:, cols[h]] for h in range(n_heads)],
                         [acc_ref[:, cols[h]] for h in range(n_heads)])
        m_new, p_new = [], []
        for h in range(n_heads):
            d = jnp.minimum(m_ref[:, cols[h]] - mx_ref[:, cols[h]], 0.0)
            m_h = m_ref[:, cols[h]] - d
            acc_new[h] = acc_new[h] * jnp.exp2(d)
            p_new.append(jnp.exp2(s_ref[:, cols[h]] - m_h).astype(BF16))
            m_new.append(m_h)
        s_next, carry = scores(jnp.minimum(c + 1, n_chunks - 1), carry)
        for h in range(n_heads):
            acc_ref[:, cols[h]] = acc_new[h]
            m_ref[:, cols[h]] = m_new[h]
            p_ref[:, cols[h]] = p_new[h]
        put_scores(s_next)
        return carry

    lax.fori_loop(0, n_chunks, body, carry)
    return values(n_chunks - 1, [p_ref[:, cols[h]] for h in range(n_heads)],
                  [acc_ref[:, cols[h]] for h in range(n_heads)])


def _softmax_once_t(q_ref, n_heads, kvs, ovts, biases):
    per_group = n_heads // len(kvs)
    tq = q_ref.shape[0] // n_heads
    out = []
    for j, (kv, ovt, bias) in enumerate(zip(kvs, ovts, biases)):
        s_all = _dot_nt(kv, q_ref[j * per_group * tq:(j + 1) * per_group * tq])
        p = []
        for r in range(per_group):
            s = s_all[:, r * tq:(r + 1) * tq] + bias
            p.append(jnp.exp2(s - jnp.max(s, axis=0, keepdims=True)).astype(BF16))
        pv = _dot(ovt, jnp.concatenate(p, axis=1))
        out.extend(pv[:, r * tq:(r + 1) * tq] for r in range(per_group))
    return out


def _flash_scratch_t(n_heads, tq, tk):
    return [pltpu.VMEM((tk, n_heads * tq), F32), pltpu.VMEM((1, n_heads * tq), F32),
            pltpu.VMEM((tk, n_heads * tq), BF16), pltpu.VMEM((1, n_heads * tq), F32),
            pltpu.VMEM((LANES, n_heads * tq), F32)]


def _flash_finish_t(accs):
    tq = accs[0].shape[1]
    outs = []
    for p in range(len(accs) // 2):
        a0, a1 = accs[2 * p], accs[2 * p + 1]
        num = jnp.concatenate([a0[HEAD_DIM:], a1[HEAD_DIM:]], axis=0)
        den = jnp.concatenate([jnp.broadcast_to(a0[0:1], (HEAD_DIM, tq)),
                               jnp.broadcast_to(a1[0:1], (HEAD_DIM, tq))], axis=0)
        outs.append((num / den).T)
    return jnp.concatenate(outs, axis=1)


_ROPE_Q, _ROPE_IQ, _ROPE_K, _ROPE_KV = 0, 1, 2, 3


def _proj_kernel(x_ref, g_ref, w_ref, tab_ref, qa_ref, ck0_ref, ck1_ref, cv0_ref, cv1_ref, kvs_ref, ovs_ref,
                 kvw_ref, ovw_ref, qb_ref, kvb_ref, ovb_ref, iq_ref, ik_ref, misc_ref, gm_ref, kc_ref, vc_ref):
    x = x_ref[...]
    tm = x.shape[0]
    ms = jnp.mean(x * x, axis=-1, keepdims=True)
    xn = (x * lax.rsqrt(ms + NORM_EPS) * g_ref[...]).astype(BF16)
    lane = lax.broadcasted_iota(jnp.int32, (tm, LANES), 1)
    low32 = (lane & (HEAD_DIM // 2)) == 0
    low64 = lane < HEAD_DIM

    def rope(z, kind):
        cos = tab_ref[2 * kind]
        sin = tab_ref[2 * kind + 1]
        rot = jnp.where(low32, pltpu.roll(z, LANES - HEAD_DIM // 2, 1), pltpu.roll(z, HEAD_DIM // 2, 1))
        return z * cos + rot * sin

    def emit_slabs(col, slabs):
        z = _dot(xn, w_ref[:, col:col + LANES * len(slabs)])
        for j, (out_ref, out_col, kind, ones_ref) in enumerate(slabs):
            zj = z[:, LANES * j:LANES * (j + 1)]
            if kind is not None:
                zj = rope(zj, kind)
            out_ref[:, out_col:out_col + LANES] = zj.astype(out_ref.dtype)
            if ones_ref is not None:
                ones_ref[:, out_col:out_col + LANES] = jnp.where(low64, 1.0, zj).astype(ones_ref.dtype)

    def emit(col, width, out_ref, out_col, kinds, ones_ref=None):
        emit_slabs(col, [(out_ref, out_col + LANES * j, kinds[j], ones_ref) for j in range(width // LANES)])

    emit(_N_QA, 512, qa_ref, 0, [_ROPE_Q] * 4)
    emit_slabs(_N_KVC, [(kc_ref, 0, _ROPE_K, None), (vc_ref, 0, None, None)])
    n_ck = tm // CMP_STRIDE
    low64c = lax.broadcasted_iota(jnp.int32, (n_ck, LANES), 1) < HEAD_DIM
    for src_ref, c_h0, c_h1 in ((kc_ref, ck0_ref, ck1_ref), (vc_ref, cv0_ref, cv1_ref)):
        for i in range(CMP_STRIDE // 2):
            ta = src_ref[pl.ds(2 * i, n_ck, stride=CMP_STRIDE), :]
            tb = src_ref[pl.ds(2 * i + 1, n_ck, stride=CMP_STRIDE), :]
            cols = slice(LANES * i, LANES * (i + 1))
            c_h0[:, cols] = jnp.where(low64c, ta, pltpu.roll(tb, HEAD_DIM, 1))
            c_h1[:, cols] = jnp.where(low64c, pltpu.roll(ta, HEAD_DIM, 1), tb)
    emit(_N_KVS, 256, kvs_ref, 0, [_ROPE_KV] * 2, ovs_ref)
    emit(_N_KVW, 256, kvw_ref, 0, [_ROPE_KV] * 2, ovw_ref)
    emit(_N_QB, 512, qb_ref, 0, [_ROPE_Q] * 4)
    emit_slabs(_N_KVB, [(kvb_ref, 0, _ROPE_KV, ovb_ref), (ik_ref, 0, _ROPE_K, None)])
    emit(_N_IQ, 256, iq_ref, 0, [_ROPE_IQ] * 2)
    emit(_N_MISC, 128, misc_ref, 0, [None])
    for c in range(4):
        emit(_N_GM + 512 * c, 512, gm_ref, 512 * c, [None] * 4)


def _relayout_w_in(w):
    z = lambda n: jnp.zeros((w.shape[0], n), w.dtype)
    s = lambda a, n: w[:, a:a + n]
    cols = [
        s(_O_QA, 512), s(_O_KC, 128), s(_O_VC, 128),
        s(_O_KSL, 64), s(_O_VSL, 64), s(_O_KSL + 64, 64), s(_O_VSL + 64, 64),
        s(_O_KWN, 64), s(_O_VWN, 64), s(_O_KWN + 64, 64), s(_O_VWN + 64, 64),
        s(_O_QB, 512), s(_O_KB, 64), s(_O_VB, 64),
        s(_O_IK, 64), z(64), s(_O_IQ, 256),
        s(_O_GA, 24), s(_O_IW, 4), z(100),
        s(_O_GM, 2048),
    ]
    out = jnp.concatenate(cols, axis=1)
    assert out.shape[1] == _D_IN_PAD
    return out.astype(BF16)


def _rope_tables(seq):
    half = HEAD_DIM // 2
    inv_freq = ROPE_THETA ** (-jnp.arange(half, dtype=F32) / half)
    ang = jnp.arange(seq, dtype=F32)[:, None] * inv_freq[None, :]
    cos, sin = jnp.cos(ang), jnp.sin(ang)
    cos64 = jnp.concatenate([cos, cos], axis=1)
    sin64 = jnp.concatenate([-sin, sin], axis=1)
    cos_k = jnp.concatenate([cos64, cos64], axis=1)
    sin_k = jnp.concatenate([sin64, sin64], axis=1)
    scale = HEAD_DIM ** -0.5
    q_scale = scale * math.log2(math.e)
    cos_kv = jnp.concatenate([cos64, jnp.ones_like(cos64)], axis=1)
    sin_kv = jnp.concatenate([sin64, jnp.zeros_like(sin64)], axis=1)
    return jnp.stack([cos_k * q_scale, sin_k * q_scale, cos_k * scale, sin_k * scale,
                      cos_k, sin_k, cos_kv, sin_kv], axis=0)


def _proj(x2, gain, w_pad, tabs, seq, tm):
    n = x2.shape[0]
    nblk_seq = seq // tm
    row = lambda width: pl.BlockSpec((tm, width), lambda i: (i, 0))
    full = lambda shape: pl.BlockSpec(shape, lambda i: (0,) * len(shape))
    outs = [(1, 512, BF16)] + [(CMP_STRIDE, CMP_STRIDE * HEAD_DIM, F32)] * 4 + [
        (1, 256, BF16), (1, 256, BF16), (1, 256, BF16), (1, 256, BF16), (1, 512, BF16),
        (1, 128, BF16), (1, 128, BF16), (1, 256, BF16), (1, 128, BF16), (1, 128, F32), (1, 2048, F32)]
    return pl.pallas_call(
        _proj_kernel,
        grid=(n // tm,),
        in_specs=[row(D_MODEL), full((1, D_MODEL)), full((D_MODEL, _D_IN_PAD)),
                  pl.BlockSpec((tabs.shape[0], tm, LANES), lambda i: (0, i % nblk_seq, 0))],
        out_specs=[pl.BlockSpec((tm // div, wd), lambda i: (i, 0)) for div, wd, _ in outs],
        out_shape=[jax.ShapeDtypeStruct((n // div, wd), dt) for div, wd, dt in outs],
        scratch_shapes=[pltpu.VMEM((tm, LANES), F32), pltpu.VMEM((tm, LANES), F32)],
        compiler_params=pltpu.CompilerParams(dimension_semantics=("arbitrary",), vmem_limit_bytes=VMEM_LIMIT),
    )(x2, gain, w_pad, tabs)


def _compress_kernel(ck0_ref, ck1_ref, cv0_ref, cv1_ref, pos_ref, w1_ref, w2_ref, out_ref):
    c_refs = ((ck0_ref, ck1_ref), (cv0_ref, cv1_ref))
    n_chunk = ck0_ref.shape[0]
    for h in range(NSA_KV_HEADS):
        acc = jnp.zeros((n_chunk, LANES), F32)
        for kv in range(2):
            c = c_refs[kv][h][...]
            a_lo = (c + pos_ref[kv, 0]).astype(BF16)
            a_hi = (c + pos_ref[kv, 1]).astype(BF16)
            half = CMP_STRIDE * HEAD_DIM
            h_lo = _dot(a_lo, w1_ref[kv, :half, :])
            h_hi = _dot(a_hi, w1_ref[kv, half:, :])
            hid = h_lo + pltpu.roll(h_hi, n_chunk - 1, 0)
            act = jax.nn.gelu(hid, approximate=True).astype(BF16)
            acc = acc + _dot(act, w2_ref[kv])
        out_ref[0, h] = acc.astype(out_ref.dtype)


def _compress(cs, b, pos, w1, w2):
    n_chunk = cs[0].shape[0] // b
    return pl.pallas_call(
        _compress_kernel,
        grid=(b,),
        in_specs=[pl.BlockSpec((n_chunk, c.shape[1]), lambda i: (i, 0)) for c in cs] + [
                  pl.BlockSpec(pos.shape, lambda i: (0, 0, 0, 0)),
                  pl.BlockSpec(w1.shape, lambda i: (0, 0, 0)),
                  pl.BlockSpec(w2.shape, lambda i: (0, 0, 0))],
        out_specs=pl.BlockSpec((1, NSA_KV_HEADS, n_chunk, LANES), lambda i: (i, 0, 0, 0)),
        out_shape=jax.ShapeDtypeStruct((b, NSA_KV_HEADS, n_chunk, LANES), BF16),
        compiler_params=pltpu.CompilerParams(dimension_semantics=("arbitrary",), vmem_limit_bytes=VMEM_LIMIT),
    )(*cs, pos, w1, w2)


def _cmp_branch(q_ref, kvc_ref, ot_ref, ocmp_ref, sel_ref, q0, tq):
    n_cmp = kvc_ref.shape[2]
    n_slc = ot_ref.shape[0]
    t3 = q0 + lax.broadcasted_iota(jnp.int32, (NSA_GROUP, tq, n_cmp), 1)
    n3 = lax.broadcasted_iota(jnp.int32, (NSA_GROUP, tq, n_cmp), 2)
    mask = (n3 * CMP_STRIDE + (CMP_BLOCK - 1)) <= t3
    j = lax.broadcasted_iota(jnp.int32, (n_slc, tq), 0)
    cur = (q0 + lax.broadcasted_iota(jnp.int32, (n_slc, tq), 1)) // SLC_BLOCK
    forced = (j == 0) | ((cur - j >= 0) & (cur - j < SLC_LOCAL))
    adm = j <= cur
    for g in range(NSA_KV_HEADS):
        gw = NSA_GROUP * HEAD_DIM
        q4 = q_ref[g * NSA_GROUP * tq:(g + 1) * NSA_GROUP * tq]
        kv = kvc_ref[0, g]
        s = _dot_nt(q4, kv).reshape(NSA_GROUP, tq, n_cmp)
        s = jnp.where(mask, s, NEG_INF)
        m = jnp.max(s, axis=2, keepdims=True)
        e = jnp.where(mask, jnp.exp2(s - m), 0.0)
        l = jnp.sum(e, axis=2, keepdims=True)
        p = e * jnp.where(l > 0.0, 1.0 / l, 0.0)
        o = _dot(p.reshape(NSA_GROUP * tq, n_cmp).astype(BF16), kv)
        ocmp_ref[:, gw * g:gw * (g + 1)] = _unstack_heads(o, NSA_GROUP)
        psum = p[0] + p[1] + p[2] + p[3]
        hi = psum.astype(BF16)
        lo = (psum - hi.astype(F32)).astype(BF16)
        p_slc = _dot_nt(ot_ref[...], hi) + _dot_nt(ot_ref[...], lo)
        blk = jnp.where(forced, BIG, p_slc)
        blk = jnp.where(adm, blk, NEG_INF)
        rank = jnp.zeros((n_slc, tq), F32)
        for i in range(n_slc):
            bi = blk[i:i + 1, :]
            beats = (bi > blk) | ((bi == blk) & (j > i))
            rank = rank + jnp.where(beats, 1.0, 0.0)
        selt = jnp.where((rank < float(SLC_TOPN)) & adm, 1.0, 0.0)
        sel_ref[g] = jnp.concatenate([selt, jnp.zeros((LANES - n_slc, tq), F32)], axis=0).astype(BF16)


def _nsa_kernel(qa_ref, kvc_ref, ot_ref, kvs_ref, ovst_ref, kvw_ref, ovwt_ref, e_ref, misc_ref, gx_ref, o_ref,
                q_ref, sel_ref, ocmp_ref, *flash):
    tk = flash[0].shape[0]
    tq = qa_ref.shape[1]
    qi = pl.program_id(1)
    q0 = qi * tq
    span = WINDOW + tq
    gsig = _sigmoid(misc_ref[0])
    key_s = lax.broadcasted_iota(jnp.int32, (tk, tq), 0)
    t_s = q0 + lax.broadcasted_iota(jnp.int32, (tk, tq), 1)
    hw = NSA_HEADS * HEAD_DIM
    groups = range(NSA_KV_HEADS)
    cols = [slice(LANES * g, LANES * (g + 1)) for g in groups]
    g_hi = gsig.astype(BF16)
    g_lo = (gsig - g_hi.astype(F32)).astype(BF16)
    gates = _dot(g_hi, gx_ref[...]) + _dot(g_lo, gx_ref[...])

    def gate(branch):
        return gates[:, hw * branch:hw * (branch + 1)]

    q_ref[...] = _stack_heads(qa_ref[0].astype(F32), NSA_HEADS)
    _cmp_branch(q_ref, kvc_ref, ot_ref, ocmp_ref, sel_ref, q0, tq)

    def slc_kv(kt):
        k0 = pl.multiple_of(kt * tk, tk)
        return ([kvs_ref[0, pl.ds(k0, tk), cols[g]] for g in groups],
                [ovst_ref[0, kt, cols[g], :] for g in groups])

    def slc_bias(kt, carry):
        causal = (kt * tk + key_s) <= t_s
        return [jnp.where((_dot(e_ref[kt], sel_ref[g]) > 0.5) & causal, 0.0, NEG_INF) for g in groups], carry

    o_slc = _flash_finish_t(_flash_loop_t((q0 + tq + tk - 1) // tk, NSA_HEADS, q_ref, slc_kv, slc_bias, 0, flash))

    k_lo = pl.multiple_of(jnp.maximum(q0 + tq - span, 0), tk)
    c_lo = k_lo // tk
    dist = (q0 + lax.broadcasted_iota(jnp.int32, (span, tq), 1)) - (k_lo + lax.broadcasted_iota(
        jnp.int32, (span, tq), 0))
    bias_w = jnp.where((dist >= 0) & (dist < WINDOW), 0.0, NEG_INF)
    o_win = _flash_finish_t(_softmax_once_t(
        q_ref, NSA_HEADS, [kvw_ref[0, pl.ds(k_lo, span), cols[g]] for g in groups],
        [jnp.concatenate([ovwt_ref[0, c_lo + i, cols[g], :] for i in range(span // tk)], axis=1) for g in groups],
        [bias_w] * NSA_KV_HEADS))

    o_ref[0] = (gate(0) * ocmp_ref[...] + gate(1) * o_slc + gate(2) * o_win).astype(o_ref.dtype)


def _nsa(qa, kvc, ot, kvs, ovs, kvw, ovw, e, misc, tq, tk):
    b, s, _ = qa.shape
    n_cmp = kvc.shape[2]
    hw = NSA_HEADS * HEAD_DIM
    assert WINDOW % tq == 0 and WINDOW + tq <= s
    gx = np.zeros((LANES, 3 * hw), np.float32)
    for branch in range(3):
        for h in range(NSA_HEADS):
            c0 = branch * hw + HEAD_DIM * h
            gx[_MISC_GA + branch * NSA_HEADS + h, c0:c0 + HEAD_DIM] = 1.0
    gx = jnp.asarray(gx, BF16)
    assert tq == tk, "the window step addresses its value operand in whole key chunks"
    seq = lambda: pl.BlockSpec((1, s, 2 * LANES), lambda i, j: (i, 0, 0))
    seq_t = lambda: pl.BlockSpec((1, s // tk, 2 * LANES, tk), lambda i, j: (i, 0, 0, 0))
    ovs, ovw = (a.reshape(b, s // tk, tk, 2 * LANES).transpose(0, 1, 3, 2) for a in (ovs, ovw))
    return pl.pallas_call(
        _nsa_kernel,
        grid=(b, s // tq),
        in_specs=[pl.BlockSpec((1, tq, hw), lambda i, j: (i, j, 0)),
                  pl.BlockSpec((1, NSA_KV_HEADS, n_cmp, LANES), lambda i, j: (i, 0, 0, 0)),
                  pl.BlockSpec(ot.shape, lambda i, j: (0, 0)),
                  seq(), seq_t(), seq(), seq_t(),
                  pl.BlockSpec(e.shape, lambda i, j: (0, 0, 0)),
                  pl.BlockSpec((1, tq, LANES), lambda i, j: (i, j, 0)),
                  pl.BlockSpec(gx.shape, lambda i, j: (0, 0))],
        out_specs=pl.BlockSpec((1, tq, hw), lambda i, j: (i, j, 0)),
        out_shape=jax.ShapeDtypeStruct((b, s, hw), BF16),
        scratch_shapes=[pltpu.VMEM((NSA_HEADS * tq, LANES), BF16),
                        pltpu.VMEM((NSA_KV_HEADS, LANES, tq), BF16),
                        pltpu.VMEM((tq, hw), F32)] + _flash_scratch_t(NSA_HEADS, tq, tk),
        compiler_params=pltpu.CompilerParams(dimension_semantics=("arbitrary", "arbitrary"),
                                             vmem_limit_bytes=VMEM_LIMIT),
    )(qa, kvc, ot, kvs, ovs, kvw, ovw, e, misc, gx)


def _dsa_kernel(qb_ref, kvb_ref, ovt_ref, iq_ref, ik_ref, misc_ref, l2_ref, o_ref,
                score_ref, q_ref, *flash, topk, n_bisect):
    tq = qb_ref.shape[1]
    ts = score_ref.shape[1]
    ta = 2 * ts
    fold = ts // SUBLANES
    qi = pl.program_id(1)
    q0 = qi * tq
    n_pair = (q0 + tq + ta - 1) // ta
    key = lax.broadcasted_iota(jnp.int32, (ts, tq), 0)
    qry = lax.broadcasted_iota(jnp.int32, (ts, tq), 1)

    def fold_min(x):
        return jnp.min(x.reshape(fold, SUBLANES, tq), axis=0)

    def fold_max(x):
        return jnp.max(x.reshape(fold, SUBLANES, tq), axis=0)

    def fold_sum(x):
        return jnp.sum(x.reshape(fold, SUBLANES, tq), axis=0)

    q_ref[0:IDX_HEADS * tq] = _stack_heads(iq_ref[0].astype(F32), IDX_HEADS)
    misc_t = misc_ref[0].T
    wts = [misc_t[_MISC_IW + h:_MISC_IW + h + 1, :] * (IDX_HEADS ** -0.5) for h in range(IDX_HEADS)]

    def score_body(c, carry):
        mn, mx = carry
        k0 = pl.multiple_of(c * 4 * ts, 4 * ts)
        lg = _dot_nt(ik_ref[0, pl.ds(k0, 4 * ts), :], q_ref[0:IDX_HEADS * tq])
        for j in range(4):
            rows = slice(j * ts, (j + 1) * ts)
            sc = wts[0] * jnp.maximum(lg[rows, 0:tq], 0.0)
            for h in range(1, IDX_HEADS):
                sc = sc + wts[h] * jnp.maximum(lg[rows, h * tq:(h + 1) * tq], 0.0)
            causal = (k0 + j * ts + key) <= (q0 + qry)
            score_ref[4 * c + j] = jnp.where(causal, sc, NEG_INF)
            mn = jnp.minimum(mn, fold_min(jnp.where(causal, sc, _FMAX)))
            mx = jnp.maximum(mx, fold_max(jnp.where(causal, sc, -_FMAX)))
        return mn, mx

    mn, mx = lax.fori_loop(0, (n_pair + 1) // 2, score_body, (jnp.full((SUBLANES, tq), _FMAX, F32),
                                                              jnp.full((SUBLANES, tq), -_FMAX, F32)))
    lo = jnp.min(mn, axis=0, keepdims=True)
    hi = jnp.max(mx, axis=0, keepdims=True)
    kf = jnp.minimum(q0 + lax.broadcasted_iota(jnp.int32, (1, tq), 1) + 1, topk).astype(F32)

    def midpoint(lo, hi):
        mid = lo + (hi - lo) * 0.5
        return jnp.where(mid >= hi, lo, mid)

    def count_gt(thr):
        def body(c, cnt):
            for j in range(4):
                cnt = cnt + fold_sum(jnp.where(score_ref[4 * c + j] > thr, 1.0, 0.0))
            return cnt
        cnt = lax.fori_loop(0, (n_pair + 1) // 2, body, jnp.zeros((SUBLANES, tq), F32))
        return jnp.sum(cnt, axis=0, keepdims=True)

    def bisect(_, carry):
        lo, hi = carry
        mid = midpoint(lo, hi)
        up = count_gt(mid) >= kf
        return jnp.where(up, mid, lo), jnp.where(up, hi, mid)

    lo, hi = lax.fori_loop(0, n_bisect, bisect, (lo, hi))

    def snap_cond(carry):
        lo, hi = carry
        return jnp.max(hi - lo) > 0.0

    def snap(carry):
        lo, hi = carry
        mid = midpoint(lo, hi)

        def body(c, carry):
            cnt, above, below = carry
            for j in range(2):
                sc = score_ref[2 * c + j]
                gt = sc > mid
                cnt = cnt + fold_sum(jnp.where(gt, 1.0, 0.0))
                above = jnp.minimum(above, fold_min(jnp.where(gt, sc, _FMAX)))
                below = jnp.maximum(below, fold_max(jnp.where(gt, -_FMAX, sc)))
            return cnt, above, below

        cnt, above, below = lax.fori_loop(
            0, n_pair, body, (jnp.zeros((SUBLANES, tq), F32), jnp.full((SUBLANES, tq), _FMAX, F32),
                              jnp.full((SUBLANES, tq), -_FMAX, F32)))
        up = jnp.sum(cnt, axis=0, keepdims=True) >= kf
        return (jnp.where(up, jnp.min(above, axis=0, keepdims=True), lo),
                jnp.where(up, hi, jnp.max(below, axis=0, keepdims=True)))

    v, _ = lax.while_loop(snap_cond, lambda carry: snap(snap(carry)), (lo, hi))
    need = kf - count_gt(v)

    q_ref[...] = _stack_heads(qb_ref[0].astype(F32), DSA_HEADS)

    def att_kv(c):
        k0 = pl.multiple_of(c * ta, ta)
        return [kvb_ref[0, pl.ds(k0, ta), :]], [ovt_ref[0, c]]

    def att_bias(c, run):
        halves = []
        for j in range(2):
            sc = score_ref[2 * c + j]
            eq = sc == v
            pt = _dot(l2_ref[...], jnp.where(eq, 1.0, 0.0).astype(BF16))
            take = (sc > v) | (eq & (run + pt[:ts] < need))
            halves.append(jnp.where(take, 0.0, NEG_INF))
            run = run + pt[ts:ts + 1]
        return [jnp.concatenate(halves, axis=0)], run

    accs = _flash_loop_t(n_pair, DSA_HEADS, q_ref, att_kv, att_bias, jnp.zeros((1, tq), F32), flash)
    o_ref[0] = _flash_finish_t(accs).astype(o_ref.dtype)


def _dsa(qb, kvb, ovb, iq, ik, misc, l2, tq, topk, n_bisect):
    b, s, _ = qb.shape
    hw = DSA_HEADS * HEAD_DIM
    ts = l2.shape[1]
    ta = 2 * ts
    assert (s // ts) % 4 == 0 and tq % ta == 0
    ovt = ovb.reshape(b, s // ta, ta, LANES).transpose(0, 1, 3, 2)
    return pl.pallas_call(
        functools.partial(_dsa_kernel, topk=topk, n_bisect=n_bisect),
        grid=(b, s // tq),
        in_specs=[pl.BlockSpec((1, tq, hw), lambda i, j: (i, j, 0)),
                  pl.BlockSpec((1, s, LANES), lambda i, j: (i, 0, 0)),
                  pl.BlockSpec((1, s // ta, LANES, ta), lambda i, j: (i, 0, 0, 0)),
                  pl.BlockSpec((1, tq, IDX_HEADS * IDX_DIM), lambda i, j: (i, j, 0)),
                  pl.BlockSpec((1, s, LANES), lambda i, j: (i, 0, 0)),
                  pl.BlockSpec((1, tq, LANES), lambda i, j: (i, j, 0)),
                  pl.BlockSpec(l2.shape, lambda i, j: (0, 0))],
        out_specs=pl.BlockSpec((1, tq, hw), lambda i, j: (i, j, 0)),
        out_shape=jax.ShapeDtypeStruct((b, s, hw), BF16),
        scratch_shapes=[pltpu.VMEM((s // ts, ts, tq), F32),
                        pltpu.VMEM((DSA_HEADS * tq, LANES), BF16)] + _flash_scratch_t(DSA_HEADS, tq, ta),
        compiler_params=pltpu.CompilerParams(dimension_semantics=("arbitrary", "arbitrary"),
                                             vmem_limit_bytes=VMEM_LIMIT),
    )(qb, kvb, ovt, iq, ik, misc, l2)


def _post_kernel(on_ref, od_ref, gm_ref, x_ref, wbn_ref, wbd_ref, wo_ref, gffn_ref, wg_ref, wu_ref, wd_ref,
                 gfin_ref, out_ref):
    ya = _dot(on_ref[...], wbn_ref[...])
    yb = _dot(od_ref[...], wbd_ref[...])
    gm = gm_ref[...]
    merged = _sigmoid(gm[:, :D_MODEL]) * ya + _sigmoid(gm[:, D_MODEL:]) * yb
    h = x_ref[...] + _dot(merged.astype(BF16), wo_ref[...])
    ms = jnp.mean(h * h, axis=-1, keepdims=True)
    hn = (h * lax.rsqrt(ms + NORM_EPS) * gffn_ref[...]).astype(BF16)
    acc = jnp.zeros_like(h)
    fc = 2 * LANES
    for c in range(wg_ref.shape[1] // fc):
        cols = slice(c * fc, (c + 1) * fc)
        gt = _dot(hn, wg_ref[:, cols])
        up = _dot(hn, wu_ref[:, cols])
        act = (gt * _sigmoid(gt) * up).astype(BF16)
        acc = acc + _dot(act, wd_ref[cols, :])
    h2 = h + acc
    ms2 = jnp.mean(h2 * h2, axis=-1, keepdims=True)
    out_ref[...] = h2 * lax.rsqrt(ms2 + NORM_EPS) * gfin_ref[...]


def _post(o_nsa, o_dsa, gm, x2, wbn, wbd, wo, gffn, wg, wu, wd, gfin, tm):
    n = x2.shape[0]
    row = lambda width: pl.BlockSpec((tm, width), lambda i: (i, 0))
    full = lambda a: pl.BlockSpec(a.shape, lambda i: (0,) * a.ndim, pipeline_mode=pl.Buffered(1))
    return pl.pallas_call(
        _post_kernel,
        grid=(n // tm,),
        in_specs=[row(o_nsa.shape[1]), row(o_dsa.shape[1]), row(2 * D_MODEL), row(D_MODEL),
                  full(wbn), full(wbd), full(wo), full(gffn), full(wg), full(wu), full(wd), full(gfin)],
        out_specs=row(D_MODEL),
        out_shape=jax.ShapeDtypeStruct((n, D_MODEL), F32),
        compiler_params=pltpu.CompilerParams(dimension_semantics=("arbitrary",), vmem_limit_bytes=VMEM_LIMIT),
    )(o_nsa, o_dsa, gm, x2, wbn, wbd, wo, gffn, wg, wu, wd, gfin)


def _layer(h2, b, s, norm_mix, w_in, cmp_pos_k, cmp_w1_k, cmp_w2_k, cmp_pos_v, cmp_w1_v, cmp_w2_v,
           w_branch_nsa, w_branch_dsa, w_out, norm_ffn, w_gate, w_up, w_down, norm_out):
    n_chunk = s // CMP_STRIDE
    n_slc = s // SLC_BLOCK
    tabs = _rope_tables(s)
    qa, ck0, ck1, cv0, cv1, kvs, ovs, kvw, ovw, qb, kvb, ovb, iq, ik, misc, gm = _proj(
        h2, norm_mix[None, :], _relayout_w_in(w_in), tabs, s, _TM)
    seq3 = lambda a: a.reshape(b, s, -1)

    pos = jnp.stack([cmp_pos_k, cmp_pos_v]).reshape(2, 2, 1, CMP_STRIDE * HEAD_DIM)
    w1 = jnp.stack([cmp_w1_k, cmp_w1_v]).astype(BF16)
    zpad = jnp.zeros((CMP_HIDDEN, HEAD_DIM), F32)
    w2 = jnp.stack([jnp.concatenate([cmp_w2_k, zpad], axis=1),
                    jnp.concatenate([zpad, cmp_w2_v], axis=1)]).astype(BF16)
    kvcmp = _compress((ck0, ck1, cv0, cv1), b, pos, w1, w2)

    ci = np.arange(n_chunk)[None, :] * CMP_STRIDE
    sj = np.arange(n_slc)[:, None] * SLC_BLOCK
    ot = ((ci < sj + SLC_BLOCK) & (ci + CMP_BLOCK > sj) & (np.arange(n_chunk)[None, :] < n_chunk - 1))

    assert n_slc <= LANES
    e = (np.arange(s)[None, :] // SLC_BLOCK == np.arange(LANES)[:, None])
    e = jnp.asarray(e.reshape(LANES, s // _TK, _TK).transpose(1, 2, 0), BF16)
    o_nsa = _nsa(seq3(qa), kvcmp, jnp.asarray(ot, BF16), seq3(kvs), seq3(ovs), seq3(kvw), seq3(ovw), e,
                 seq3(misc), _TQ, _TK)

    idx = np.arange(LANES)
    l2 = jnp.asarray(np.concatenate([idx[:, None] > idx[None, :], np.ones((LANES, LANES), bool)], axis=0), BF16)
    o_dsa = _dsa(seq3(qb), seq3(kvb), seq3(ovb), seq3(iq), seq3(ik), seq3(misc), l2, _TQ,
                 min(DSA_TOPK_MAX, s // 4), _N_BISECT)

    wg, wu, wd = w_gate.astype(BF16), w_up.astype(BF16), w_down.astype(BF16)
    return _post(o_nsa.reshape(b * s, -1), o_dsa.reshape(b * s, -1), gm, h2,
                 w_branch_nsa.astype(BF16), w_branch_dsa.astype(BF16), w_out.astype(BF16),
                 norm_ffn[None, :], wg, wu, wd, norm_out[None, :], _TM)


def kernel(x, norm_mix, w_in, cmp_pos_k, cmp_w1_k, cmp_w2_k, cmp_pos_v, cmp_w1_v, cmp_w2_v, w_branch_nsa,
           w_branch_dsa, w_out, norm_ffn, w_gate, w_up, w_down, norm_final):
    b, s, d = x.shape
    depth = norm_mix.shape[0]
    assert depth == 1, "the fused epilogue applies the final norm right after the single layer"
    out = _layer(x.reshape(b * s, d), b, s, norm_mix[0], w_in[0], cmp_pos_k[0], cmp_w1_k[0], cmp_w2_k[0],
                 cmp_pos_v[0], cmp_w1_v[0], cmp_w2_v[0], w_branch_nsa[0], w_branch_dsa[0], w_out[0],
                 norm_ffn[0], w_gate[0], w_up[0], w_down[0], norm_final)
    return out.reshape(b, s, d)
```
